```python
import jax, jax.numpy as jnp
from jax import lax
import numpy as np

D_MODEL = 2048
BATCH = 4
SEQ = 2048
DEPTH = 4
DEC_BATCH = 8
DEC_SEQ = 1
PAST_LEN = 16384
PAGE_SIZE = 128

N_A_LAYERS = DEPTH // 2
N_B_LAYERS = DEPTH - N_A_LAYERS
CONV_W = 3
N_HEADS = 16
HEAD_DIM = D_MODEL // N_HEADS
N_GROUPS = 4
EXPERTS_PER_GROUP = 4
N_EXPERTS = N_GROUPS * EXPERTS_PER_GROUP
TOP_K = 2
D_EXPERT = D_MODEL // 2
Q_BLOCK = 128
RMS_EPS = 1e-6
SB_BIAS_INIT = -6.0

kernel_name = "yoco_shortconv_stickbreaking_hmoe_step"


def _rmsnorm(x, g):
    xf = x.astype(jnp.float32)
    var = jnp.mean(xf * xf, axis=-1, keepdims=True)
    return (xf * lax.rsqrt(var + RMS_EPS)).astype(x.dtype) * g


def _short_conv(h, buf, w_in, w_conv, w_out):
    T = h.shape[1]
    b_gate, c_gate, xin = jnp.split(h @ w_in, 3, axis=-1)
    u = c_gate * xin
    u_ext = jnp.concatenate([buf.astype(u.dtype), u], axis=1)
    conv = w_conv[0] * u_ext[:, 0:T]
    for i in range(1, CONV_W):
        conv = conv + w_conv[i] * u_ext[:, i:i + T]
    y = (b_gate * conv) @ w_out
    return y, u_ext[:, T:]


def _shared_kv(x, kv_norm, w_kv):
    B, T, _ = x.shape
    k, v = jnp.split(_rmsnorm(x, kv_norm) @ w_kv, 2, axis=-1)
    return (k.reshape(B, T, N_HEADS, HEAD_DIM), v.reshape(B, T, N_HEADS, HEAD_DIM))


def _sb_block(q, k, v, bias, q_idx, k_idx):
    z = jnp.einsum('bqhd,bkhd->bhqk', q, k).astype(jnp.float32) * (HEAD_DIM ** -0.5)
    z = z + bias.astype(jnp.float32)[None, :, None, None]
    mask = k_idx[None, :] < q_idx[:, None]
    log_stay = jnp.where(mask, jax.nn.log_sigmoid(-z), 0.0)
    log_after = lax.cumsum(log_stay, axis=3, reverse=True) - log_stay
    a = jnp.where(mask, jnp.exp(jax.nn.log_sigmoid(z) + log_after), 0.0)
    return jnp.einsum('bhqk,bkhd->bqhd', a.astype(v.dtype), v)


def _sb_prompt(q, k, v, bias):
    B, T, H, Dh = q.shape
    nb = T // Q_BLOCK
    qb = q.reshape(B, nb, Q_BLOCK, H, Dh).transpose(1, 0, 2, 3, 4)
    k_idx = jnp.arange(T, dtype=jnp.int32)
    q_idx = k_idx.reshape(nb, Q_BLOCK)
    out = lax.map(lambda a: _sb_block(a[0], k, v, bias, a[1], k_idx), (qb, q_idx))
    return out.transpose(1, 0, 2, 3, 4).reshape(B, T, H, Dh)


def _sb_sample(q, k_all, v_all, bias, past_len):
    T = q.shape[1]
    q_idx = past_len + jnp.arange(T, dtype=jnp.int32)
    k_idx = jnp.arange(k_all.shape[1], dtype=jnp.int32)
    return _sb_block(q, k_all, v_all, bias, q_idx, k_idx)


def _hmoe(h, w_g, b_g, w_e, b_e, w_gate, w_up, w_down):
    B, T, D = h.shape
    t = h.reshape(B * T, D)
    n = t.shape[0]
    g_prob = jax.nn.softmax((t @ w_g).astype(jnp.float32) + b_g, axis=-1)
    g_val, g_idx = lax.top_k(g_prob, 1)
    e_logits = ((t @ w_e).astype(jnp.float32) + b_e).reshape(n, N_GROUPS, EXPERTS_PER_GROUP)
    sel = jnp.broadcast_to(g_idx[:, :, None], (n, 1, EXPERTS_PER_GROUP))
    e_logits = jnp.take_along_axis(e_logits, sel, axis=1)[:, 0]
    e_val, e_idx = lax.top_k(jax.nn.softmax(e_logits, axis=-1), TOP_K)
    e_val = e_val / jnp.sum(e_val, axis=-1, keepdims=True)
    weights = g_val * e_val
    expert_id = g_idx * EXPERTS_PER_GROUP + e_idx
    combine = jnp.sum(jax.nn.one_hot(expert_id, N_EXPERTS, dtype=jnp.float32) * weights[..., None], axis=1)
    gate = jnp.einsum('nd,edf->enf', t, w_gate)
    up = jnp.einsum('nd,edf->enf', t, w_up)
    act = jax.nn.silu(gate) * up * combine.T[:, :, None].astype(t.dtype)
    return jnp.einsum('enf,efd->nd', act, w_down).reshape(B, T, D)


def setup_inputs(seed: int = 0) -> dict:
    key = jax.random.key(seed)
    ks = jax.random.split(key, 24)
    f32 = jnp.float32
    D, F, E, G = D_MODEL, D_EXPERT, N_EXPERTS, N_GROUPS
    n_pages = PAST_LEN // PAGE_SIZE
    n_used = DEC_BATCH * n_pages
    n_pool = n_used + max(1, n_used // 4)

    def nrm(k, shape, scale):
        return jax.random.normal(k, shape, f32) * scale

    page_table = jax.random.permutation(ks[5], n_pool)[:n_used].reshape(DEC_BATCH, n_pages).astype(jnp.int32)
    return {
        "x_prompt": nrm(ks[0], (BATCH, SEQ, D), 1.0),
        "x_sample": nrm(ks[1], (DEC_BATCH, DEC_SEQ, D), 1.0),
        "state_conv": nrm(ks[2], (N_A_LAYERS, DEC_BATCH, CONV_W - 1, D), 1.0),
        "cache_k": nrm(ks[3], (n_pool, PAGE_SIZE, N_HEADS, HEAD_DIM), 1.0),
        "cache_v": nrm(ks[4], (n_pool, PAGE_SIZE, N_HEADS, HEAD_DIM), 1.0),
        "page_table": page_table,
        "norm_mix": 1.0 + nrm(ks[6], (DEPTH, D), 0.02),
        "norm_ffn": 1.0 + nrm(ks[7], (DEPTH, D), 0.02),
        "conv_w_in": nrm(ks[8], (N_A_LAYERS, D, 3 * D), D ** -0.5),
        "conv_w": nrm(ks[9], (N_A_LAYERS, CONV_W, D), CONV_W ** -0.5),
        "conv_w_out": nrm(ks[10], (N_A_LAYERS, D, D), D ** -0.5),
        "kv_norm": 1.0 + nrm(ks[11], (D,), 0.02),
        "w_kv": nrm(ks[12], (D, 2 * D), D ** -0.5),
        "attn_w_q": nrm(ks[13], (N_B_LAYERS, D, D), D ** -0.5),
        "attn_b": SB_BIAS_INIT + nrm(ks[23], (N_B_LAYERS, N_HEADS), 0.3),
        "attn_w_o": nrm(ks[14], (N_B_LAYERS, D, D), D ** -0.5),
        "router_group_w": nrm(ks[15], (DEPTH, D, G), D ** -0.5),
        "router_group_b": nrm(ks[16], (DEPTH, G), 0.01),
        "router_expert_w": nrm(ks[17], (DEPTH, D, E), D ** -0.5),
        "router_expert_b": nrm(ks[18], (DEPTH, E), 0.01),
        "expert_w_gate": nrm(ks[19], (DEPTH, E, D, F), D ** -0.5),
        "expert_w_up": nrm(ks[20], (DEPTH, E, D, F), D ** -0.5),
        "expert_w_down": nrm(ks[21], (DEPTH, E, F, D), F ** -0.5),
        "final_norm": 1.0 + nrm(ks[22], (D,), 0.02),
    }


def reference(x_prompt, x_sample, state_conv, cache_k, cache_v, page_table,
              norm_mix, norm_ffn, conv_w_in, conv_w, conv_w_out, kv_norm, w_kv,
              attn_w_q, attn_b, attn_w_o, router_group_w, router_group_b,
              router_expert_w, router_expert_b, expert_w_gate, expert_w_up,
              expert_w_down, final_norm):
    xp, xs = x_prompt, x_sample
    bp, tp = xp.shape[0], xp.shape[1]
    bs, ts = xs.shape[0], xs.shape[1]
    n_pages = PAST_LEN // PAGE_SIZE
    buf_prompt0 = jnp.zeros((bp, CONV_W - 1, D_MODEL), xp.dtype)
    conv_p, conv_s = [], []
    kp = vp = k_all = v_all = ks_new = vs_new = None
    for l in range(DEPTH):
        if l < N_A_LAYERS:
            yp, nbp = _short_conv(_rmsnorm(xp, norm_mix[l]), buf_prompt0, conv_w_in[l], conv_w[l], conv_w_out[l])
            ys, nbs = _short_conv(_rmsnorm(xs, norm_mix[l]), state_conv[l], conv_w_in[l], conv_w[l], conv_w_out[l])
            conv_p.append(nbp)
            conv_s.append(nbs)
        else:
            j = l - N_A_LAYERS
            qp = (_rmsnorm(xp, norm_mix[l]) @ attn_w_q[j]).reshape(bp, tp, N_HEADS, HEAD_DIM)
            qs = (_rmsnorm(xs, norm_mix[l]) @ attn_w_q[j]).reshape(bs, ts, N_HEADS, HEAD_DIM)
            yp = _sb_prompt(qp, kp, vp, attn_b[j]).reshape(bp, tp, D_MODEL) @ attn_w_o[j]
            ys = _sb_sample(qs, k_all, v_all, attn_b[j], PAST_LEN).reshape(bs, ts, D_MODEL) @ attn_w_o[j]
        xp = xp + yp
        xs = xs + ys
        xp = xp + _hmoe(_rmsnorm(xp, norm_ffn[l]), router_group_w[l], router_group_b[l], router_expert_w[l],
                        router_expert_b[l], expert_w_gate[l], expert_w_up[l], expert_w_down[l])
        xs = xs + _hmoe(_rmsnorm(xs, norm_ffn[l]), router_group_w[l], router_group_b[l], router_expert_w[l],
                        router_expert_b[l], expert_w_gate[l], expert_w_up[l], expert_w_down[l])
        if l == N_A_LAYERS - 1:
            kp, vp = _shared_kv(xp, kv_norm, w_kv)
            ks_new, vs_new = _shared_kv(xs, kv_norm, w_kv)
            k_past = cache_k[page_table].reshape(bs, n_pages * PAGE_SIZE, N_HEADS, HEAD_DIM)
            v_past = cache_v[page_table].reshape(bs, n_pages * PAGE_SIZE, N_HEADS, HEAD_DIM)
            k_all = jnp.concatenate([k_past, ks_new.astype(k_past.dtype)], axis=1)
            v_all = jnp.concatenate([v_past, vs_new.astype(v_past.dtype)], axis=1)
    y_prompt = _rmsnorm(xp, final_norm)
    y_sample = _rmsnorm(xs, final_norm)
    conv_state_prompt = jnp.stack(conv_p, axis=0)
    conv_state_sample = jnp.stack(conv_s, axis=0)
    return (y_prompt, y_sample, conv_state_prompt, conv_state_sample, kp, vp, ks_new, vs_new)
```

```python
import functools

import jax
import jax.numpy as jnp
from jax import lax
from jax.experimental import pallas as pl
from jax.experimental.pallas import tpu as pltpu

F32 = jnp.float32
BF16 = jnp.bfloat16
I32 = jnp.int32

RMS_EPS = 1e-6
TOP_K = 2

LANES = 128
SUBLANES = 8
BF16_ROWS = 16
VMEM_LIMIT = 52 * 1024 * 1024

ROW_TILE = 256
GROUP_ROWS = SUBLANES


def _pick(n, pref):
    t = min(n, pref)
    while n % t:
        t //= 2
    return t


def _params(sem):
    return pltpu.CompilerParams(dimension_semantics=sem, vmem_limit_bytes=VMEM_LIMIT)


def _rms(x, g):
    var = jnp.mean(x * x, axis=-1, keepdims=True)
    return x * lax.rsqrt(var + RMS_EPS) * g


def _split_bf16(x):
    hi = x.astype(BF16)
    lo = (x - hi.astype(F32)).astype(BF16)
    return hi, lo


def _log_sigmoid(z):
    return jnp.minimum(z, 0.0) - jnp.log1p(jnp.exp(-jnp.abs(z)))


def _dot_hp(x, w):
    m = x.shape[0]
    assert m % SUBLANES == 0
    x_hi = x.astype(BF16).astype(F32)
    xx = jnp.concatenate([x_hi, x - x_hi], axis=0).astype(BF16)
    w_hi, w_lo = _split_bf16(w)
    r = (jnp.dot(xx, w_hi, preferred_element_type=F32)
         + jnp.dot(xx, w_lo, preferred_element_type=F32))
    return r[:m] + r[m:]


def _linear_kernel(*refs, has_norm, has_res, n_out, hp):
    refs = list(refs)
    x_ref = refs.pop(0)
    g_ref = refs.pop(0) if has_norm else None
    w_ref = refs.pop(0)
    r_ref = refs.pop(0) if has_res else None
    out_refs = refs[:n_out]
    if has_norm:
        h_ref = refs[n_out]

        @pl.when(pl.program_id(1) == 0)
        def _():
            h_ref[...] = _rms(x_ref[...], g_ref[...]).astype(h_ref.dtype)

        h = h_ref[...]
    else:
        h = x_ref[...]
    if hp:
        acc = _dot_hp(h.astype(F32), w_ref[...])
    else:
        acc = jnp.dot(h.astype(BF16), w_ref[...].astype(BF16), preferred_element_type=F32)
    if has_res:
        acc = acc + r_ref[...]
    for o in out_refs:
        o[...] = acc.astype(o.dtype)


def _linear(x, w3, l, *, gamma=None, res=None, col_off=0, n_cols=None,
            out_dtypes=(F32,), tm=1024, tn=512, hp=False):
    m_rows, K = x.shape
    if hp and m_rows % SUBLANES:
        pad = ((0, SUBLANES - m_rows % SUBLANES), (0, 0))
        x = jnp.pad(x, pad)
        res = None if res is None else jnp.pad(res, pad)
    M = x.shape[0]
    n_cols = n_cols or w3.shape[2]
    tm = _pick(M, tm)
    tn = _pick(n_cols, tn)
    assert col_off % tn == 0
    jo = col_off // tn
    in_specs = [pl.BlockSpec((tm, K), lambda i, j: (i, 0))]
    args = [x]
    if gamma is not None:
        in_specs.append(pl.BlockSpec((1, K), lambda i, j: (0, 0)))
        args.append(gamma.reshape(1, K))
    in_specs.append(pl.BlockSpec((None, K, tn), lambda i, j: (l, 0, j + jo)))
    args.append(w3)
    if res is not None:
        in_specs.append(pl.BlockSpec((tm, tn), lambda i, j: (i, j)))
        args.append(res)
    scratch = []
    if gamma is not None:
        scratch.append(pltpu.VMEM((tm, K), F32 if hp or tm % BF16_ROWS else BF16))
    outs = pl.pallas_call(
        functools.partial(_linear_kernel, has_norm=gamma is not None,
                          has_res=res is not None, n_out=len(out_dtypes), hp=hp),
        out_shape=[jax.ShapeDtypeStruct((M, n_cols), dt) for dt in out_dtypes],
        grid=(M // tm, n_cols // tn),
        in_specs=in_specs,
        out_specs=[pl.BlockSpec((tm, tn), lambda i, j: (i, j)) for _ in out_dtypes],
        scratch_shapes=scratch,
        compiler_params=_params(("parallel", "arbitrary")),
        name="linear",
    )(*args)
    outs = [o[:m_rows] for o in outs] if M != m_rows else outs
    return outs[0] if len(out_dtypes) == 1 else outs


def _conv_in_kernel(x_ref, g_ref, wb_ref, wc_ref, wx_ref, cw_ref, gout_ref, st_ref,
                    h_ref, carry_ref, *, tiles_per_seq, tn, conv_w):
    i = pl.program_id(0)
    j = pl.program_id(1)

    @pl.when(j == 0)
    def _():
        h_ref[...] = _rms(x_ref[...], g_ref[...]).astype(BF16)

    h = h_ref[...]
    b = jnp.dot(h, wb_ref[...].astype(BF16), preferred_element_type=F32)
    c = jnp.dot(h, wc_ref[...].astype(BF16), preferred_element_type=F32)
    xin = jnp.dot(h, wx_ref[...].astype(BF16), preferred_element_type=F32)
    u = c * xin
    tm = u.shape[0]
    cols = pl.ds(pl.multiple_of(j * tn, tn), tn)
    first = (i % tiles_per_seq) == 0
    prev = jnp.where(first, 0.0, carry_ref[:, cols])
    cw = cw_ref[...]
    row8 = lax.broadcasted_iota(I32, (SUBLANES, tn), 0)
    conv_top = cw[conv_w - 1:conv_w, :] * u[:SUBLANES]
    conv_rest = cw[conv_w - 1:conv_w, :] * u[SUBLANES:]
    for s in range(1, conv_w):
        wrow = cw[conv_w - 1 - s:conv_w - s, :]
        shifted = pltpu.roll(u, s, 0)
        top = jnp.where(row8 < s, pltpu.roll(prev, s, 0), shifted[:SUBLANES])
        conv_top = conv_top + wrow * top
        conv_rest = conv_rest + wrow * shifted[SUBLANES:]
    conv = jnp.concatenate([conv_top, conv_rest], axis=0)
    gout_ref[...] = (b * conv).astype(gout_ref.dtype)
    last8 = u[tm - SUBLANES:, :]
    carry_ref[:, cols] = last8
    st_ref[...] = last8


def _conv_in(x, gamma, w_in3, l, cw, *, seq_len, tm=1024, tn=256):
    M, D = x.shape
    conv_w = cw.shape[0]
    tm = _pick(seq_len, tm)
    tn = _pick(D, tn)
    nD = D // tn
    assert conv_w - 1 <= SUBLANES and tm >= 2 * SUBLANES
    tiles_per_seq = seq_len // tm
    g, st = pl.pallas_call(
        functools.partial(_conv_in_kernel, tiles_per_seq=tiles_per_seq, tn=tn, conv_w=conv_w),
        out_shape=[jax.ShapeDtypeStruct((M, D), BF16),
                   jax.ShapeDtypeStruct((M // tm, SUBLANES, D), F32)],
        grid=(M // tm, nD),
        in_specs=[pl.BlockSpec((tm, D), lambda i, j: (i, 0)),
                  pl.BlockSpec((1, D), lambda i, j: (0, 0)),
                  pl.BlockSpec((None, D, tn), lambda i, j: (l, 0, j)),
                  pl.BlockSpec((None, D, tn), lambda i, j: (l, 0, j + nD)),
                  pl.BlockSpec((None, D, tn), lambda i, j: (l, 0, j + 2 * nD)),
                  pl.BlockSpec((conv_w, tn), lambda i, j: (0, j))],
        out_specs=[pl.BlockSpec((tm, tn), lambda i, j: (i, j)),
                   pl.BlockSpec((None, SUBLANES, tn), lambda i, j: (i, 0, j))],
        scratch_shapes=[pltpu.VMEM((tm, D), BF16), pltpu.VMEM((SUBLANES, D), F32)],
        compiler_params=_params(("arbitrary", "arbitrary")),
        name="conv_in",
    )(x, gamma.reshape(1, D), w_in3, w_in3, w_in3, cw)
    return g, st[tiles_per_seq - 1::tiles_per_seq, SUBLANES - (conv_w - 1):, :]


def _conv_step_kernel(bcx_ref, *refs, d, conv_w, hist_is_bcx):
    hist_refs = refs[:conv_w - 1]
    cw_ref, g_ref, nst_ref = refs[conv_w - 1:]

    def conv_input(ref):
        return ref[:, d:2 * d] * ref[:, 2 * d:3 * d]

    b = bcx_ref[:, 0:d]
    u_ext = [conv_input(r) if hist_is_bcx else r[...] for r in hist_refs] + [conv_input(bcx_ref)]
    conv = cw_ref[0:1, :] * u_ext[0]
    for i in range(1, conv_w):
        conv = conv + cw_ref[i:i + 1, :] * u_ext[i]
    g_ref[...] = b * conv
    for i in range(conv_w - 1):
        nst_ref[:, i * d:(i + 1) * d] = u_ext[i + 1]


def _conv_step(bcx, hist, cw, *, hist_is_bcx):
    n = bcx.shape[0]
    conv_w, d = cw.shape
    g, nst = pl.pallas_call(
        functools.partial(_conv_step_kernel, d=d, conv_w=conv_w, hist_is_bcx=hist_is_bcx),
        out_shape=[jax.ShapeDtypeStruct((n, d), F32),
                   jax.ShapeDtypeStruct((n, (conv_w - 1) * d), F32)],
        name="conv_step",
    )(bcx, *hist, cw)
    return g, nst.reshape(n, conv_w - 1, d)


def _router_kernel(x_ref, g_ref, wt_ref, bias_ref, base_ref, ut_ref,
                   h_ref, ids_ref, rank_ref, wts_ref, cnt_ref,
                   *, n_groups, epg, n_valid, tm):
    i = pl.program_id(0)

    @pl.when(i == 0)
    def _():
        cnt_ref[...] = base_ref[...]

    hn = _rms(x_ref[...], g_ref[...])
    h_ref[...] = hn
    h_hi, h_lo = _split_bf16(hn)
    w_hi, w_lo = _split_bf16(wt_ref[...])
    nt = (((1,), (1,)), ((), ()))
    logits = (lax.dot_general(w_hi, h_hi, nt, preferred_element_type=F32)
              + lax.dot_general(w_hi, h_lo, nt, preferred_element_type=F32)
              + lax.dot_general(w_lo, h_hi, nt, preferred_element_type=F32))
    logits = logits + bias_ref[...]
    n_rows = logits.shape[0]
    rows = lax.broadcasted_iota(I32, (n_rows, tm), 0)
    big = jnp.int32(n_rows)
    neg = jnp.float32(-jnp.inf)

    is_g = rows < n_groups
    lg = jnp.where(is_g, logits, neg)
    eg = jnp.exp(lg - jnp.max(lg, axis=0, keepdims=True))
    pg = eg / jnp.sum(eg, axis=0, keepdims=True)
    g_val = jnp.max(pg, axis=0, keepdims=True)
    g_idx = jnp.min(jnp.where((pg == g_val) & is_g, rows, big), axis=0, keepdims=True)

    lo_row = GROUP_ROWS + g_idx * epg
    in_grp = (rows >= lo_row) & (rows < lo_row + epg)
    le = jnp.where(in_grp, logits, neg)
    ee = jnp.exp(le - jnp.max(le, axis=0, keepdims=True))
    pe = jnp.where(in_grp, ee / jnp.sum(ee, axis=0, keepdims=True), -1.0)
    v1 = jnp.max(pe, axis=0, keepdims=True)
    i1 = jnp.min(jnp.where(pe == v1, rows, big), axis=0, keepdims=True)
    pe2 = jnp.where(rows == i1, -1.0, pe)
    v2 = jnp.max(pe2, axis=0, keepdims=True)
    i2 = jnp.min(jnp.where(pe2 == v2, rows, big), axis=0, keepdims=True)
    den = v1 + v2
    wts_ref[0:1, :] = g_val * (v1 / den)
    wts_ref[1:2, :] = g_val * (v2 / den)
    ids_ref[0:1, :] = i1 - GROUP_ROWS
    ids_ref[1:2, :] = i2 - GROUP_ROWS

    tok = i * tm + lax.broadcasted_iota(I32, (n_rows, tm), 1)
    sel1 = rows == i1
    sel2 = rows == i2
    oh = jnp.where((sel1 | sel2) & (tok < n_valid), 1.0, 0.0)
    cum = jnp.dot(oh.astype(BF16), ut_ref[...], preferred_element_type=F32) + cnt_ref[...]
    rank_ref[0:1, :] = (jnp.sum(jnp.where(sel1, cum, 0.0), axis=0, keepdims=True) - 1.0).astype(I32)
    rank_ref[1:2, :] = (jnp.sum(jnp.where(sel2, cum, 0.0), axis=0, keepdims=True) - 1.0).astype(I32)
    cnt_ref[...] += jnp.sum(oh, axis=1, keepdims=True)


def _router(x, gamma, wt, bias, base, *, n_groups, epg, n_valid, tm=512):
    M, D = x.shape
    tm = _pick(M, tm)
    R = wt.shape[0]
    ut = jnp.triu(jnp.ones((tm, tm), BF16))
    full = lambda shape: pl.BlockSpec(shape, lambda i: (0,) * len(shape))
    tokspec = lambda dt: (jax.ShapeDtypeStruct((TOP_K, M), dt), pl.BlockSpec((TOP_K, tm), lambda i: (0, i)))
    (ids_s, ids_b), (rank_s, rank_b), (wts_s, wts_b) = tokspec(I32), tokspec(I32), tokspec(F32)
    return pl.pallas_call(
        functools.partial(_router_kernel, n_groups=n_groups, epg=epg, n_valid=n_valid, tm=tm),
        out_shape=[jax.ShapeDtypeStruct((M, D), F32), ids_s, rank_s, wts_s,
                   jax.ShapeDtypeStruct((R, 1), F32)],
        grid=(M // tm,),
        in_specs=[pl.BlockSpec((tm, D), lambda i: (i, 0)), full((1, D)), full((R, D)),
                  full((R, 1)), full((R, 1)), full((tm, tm))],
        out_specs=[pl.BlockSpec((tm, D), lambda i: (i, 0)), ids_b, rank_b, wts_b, full((R, 1))],
        compiler_params=_params(("arbitrary",)),
        name="router",
    )(x, gamma.reshape(1, D), wt, bias, base, ut)


def _dispatch_kernel(pos_ref, h_hbm, xs_in_hbm, xs_hbm, sem, *, tc, n_tok, m_pad):
    del xs_in_hbm
    base = pl.program_id(0) * tc
    n_here = jnp.clip(n_tok - base, 0, tc)

    def row_copy(tok, p):
        return pltpu.make_async_copy(h_hbm.at[pl.ds(tok, 1), :], xs_hbm.at[pl.ds(p, 1), :], sem)

    def issue(t, c):
        tok = base + t
        for k in range(TOP_K):
            row_copy(tok, pos_ref[k * m_pad + tok]).start()
        return c

    lax.fori_loop(0, n_here, issue, 0)

    def drain(t, c):
        row_copy(0, 0).wait()
        return c

    lax.fori_loop(0, TOP_K * n_here, drain, 0)


def _dispatch(h, pos, xs, *, n_tok, tc=512):
    m_pad, D = h.shape
    tc = _pick(m_pad, tc)
    return pl.pallas_call(
        functools.partial(_dispatch_kernel, tc=tc, n_tok=n_tok, m_pad=m_pad),
        out_shape=jax.ShapeDtypeStruct(xs.shape, xs.dtype),
        grid_spec=pltpu.PrefetchScalarGridSpec(
            num_scalar_prefetch=1,
            grid=(m_pad // tc,),
            in_specs=[pl.BlockSpec(memory_space=pl.ANY), pl.BlockSpec(memory_space=pl.ANY)],
            out_specs=pl.BlockSpec(memory_space=pl.ANY),
            scratch_shapes=[pltpu.SemaphoreType.DMA(())]),
        input_output_aliases={2: 0},
        compiler_params=_params(("arbitrary",)),
        name="dispatch",
    )(pos.reshape(-1), h, xs)


def _gate_up_kernel(te_ref, nu_ref, xs_ref, wg_ref, wu_ref, act_ref, wgb_ref, wub_ref):
    r = pl.program_id(1)
    used = r < nu_ref[0]
    new_expert = (r == 0) | (te_ref[r] != te_ref[jnp.maximum(r - 1, 0)])

    @pl.when(used & new_expert)
    def _():
        wgb_ref[...] = wg_ref[...].astype(BF16)
        wub_ref[...] = wu_ref[...].astype(BF16)

    @pl.when(used)
    def _():
        x = xs_ref[...].astype(BF16)
        g = jnp.dot(x, wgb_ref[...], preferred_element_type=F32)
        u = jnp.dot(x, wub_ref[...], preferred_element_type=F32)
        act_ref[...] = (g / (1.0 + jnp.exp(-g)) * u).astype(act_ref.dtype)

    @pl.when(jnp.logical_not(used))
    def _():
        act_ref[...] = jnp.zeros_like(act_ref)


def _down_kernel(te_ref, nu_ref, act_ref, wd_ref, out_ref, wdb_ref):
    r = pl.program_id(1)
    used = r < nu_ref[0]
    new_expert = (r == 0) | (te_ref[r] != te_ref[jnp.maximum(r - 1, 0)])

    @pl.when(used & new_expert)
    def _():
        wdb_ref[...] = wd_ref[...].astype(BF16)

    @pl.when(used)
    def _():
        out_ref[...] = jnp.dot(act_ref[...], wdb_ref[...], preferred_element_type=F32)

    @pl.when(jnp.logical_not(used))
    def _():
        out_ref[...] = jnp.zeros_like(out_ref)


def _experts(xs, w_gate, w_up, w_down, l, tile_expert, n_used, *, tr, tf=512, tn=1024):
    R, D = xs.shape
    F = w_gate.shape[3]
    tf = _pick(F, tf)
    tn = _pick(D, tn)
    nt = R // tr
    row = lambda f, r, te, nu: jnp.minimum(r, nu[0] - 1)
    act = pl.pallas_call(
        _gate_up_kernel,
        out_shape=jax.ShapeDtypeStruct((R, F), BF16),
        grid_spec=pltpu.PrefetchScalarGridSpec(
            num_scalar_prefetch=2,
            grid=(F // tf, nt),
            in_specs=[pl.BlockSpec((tr, D), lambda f, r, te, nu: (row(f, r, te, nu), 0)),
                      pl.BlockSpec((None, None, D, tf), lambda f, r, te, nu: (l, te[r], 0, f)),
                      pl.BlockSpec((None, None, D, tf), lambda f, r, te, nu: (l, te[r], 0, f))],
            out_specs=pl.BlockSpec((tr, tf), lambda f, r, te, nu: (r, f)),
            scratch_shapes=[pltpu.VMEM((D, tf), BF16), pltpu.VMEM((D, tf), BF16)]),
        compiler_params=_params(("arbitrary", "arbitrary")),
        name="expert_gate_up",
    )(tile_expert, n_used, xs, w_gate, w_up)
    return pl.pallas_call(
        _down_kernel,
        out_shape=jax.ShapeDtypeStruct((R, D), F32),
        grid_spec=pltpu.PrefetchScalarGridSpec(
            num_scalar_prefetch=2,
            grid=(D // tn, nt),
            in_specs=[pl.BlockSpec((tr, F), lambda n, r, te, nu: (row(n, r, te, nu), 0)),
                      pl.BlockSpec((None, None, F, tn), lambda n, r, te, nu: (l, te[r], 0, n))],
            out_specs=pl.BlockSpec((tr, tn), lambda n, r, te, nu: (r, n)),
            scratch_shapes=[pltpu.VMEM((F, tn), BF16)]),
        compiler_params=_params(("arbitrary", "arbitrary")),
        name="expert_down",
    )(tile_expert, n_used, act, w_down)


def _combine_kernel(pos_ref, x_ref, w_ref, eo_hbm, out_ref, buf_ref, sem, *, tm, m_pad):
    base = pl.program_id(0) * tm

    def row_copy(k, t, p):
        return pltpu.make_async_copy(eo_hbm.at[pl.ds(p, 1), :], buf_ref.at[k, pl.ds(t, 1), :], sem)

    def issue(t, c):
        for k in range(TOP_K):
            row_copy(k, t, pos_ref[k * m_pad + base + t]).start()
        return c

    lax.fori_loop(0, tm, issue, 0)

    def drain(t, c):
        row_copy(0, 0, 0).wait()
        return c

    lax.fori_loop(0, TOP_K * tm, drain, 0)
    w = w_ref[...]
    acc = x_ref[...]
    for k in range(TOP_K):
        acc = acc + w[:, k:k + 1] * buf_ref[k]
    out_ref[...] = acc


def _combine(x, eo, pos, wts, *, tm=256):
    M, D = x.shape
    m_pad = pos.shape[1]
    tm = _pick(M, tm)
    return pl.pallas_call(
        functools.partial(_combine_kernel, tm=tm, m_pad=m_pad),
        out_shape=jax.ShapeDtypeStruct((M, D), F32),
        grid_spec=pltpu.PrefetchScalarGridSpec(
            num_scalar_prefetch=1,
            grid=(M // tm,),
            in_specs=[pl.BlockSpec((tm, D), lambda i, p: (i, 0)),
                      pl.BlockSpec((tm, TOP_K), lambda i, p: (i, 0)),
                      pl.BlockSpec(memory_space=pl.ANY)],
            out_specs=pl.BlockSpec((tm, D), lambda i, p: (i, 0)),
            scratch_shapes=[pltpu.VMEM((TOP_K, tm, D), F32), pltpu.SemaphoreType.DMA(())]),
        compiler_params=_params(("arbitrary",)),
        name="combine",
    )(pos.reshape(-1), x, wts, eo)


def _moe(x, l, norm, wt, bias, w_gate, w_up, w_down, *, n_groups, epg):
    n_tok, D = x.shape
    n_exp = n_groups * epg
    h, ids, rank, wts, cnt = _router(x, norm, wt, bias, jnp.zeros((wt.shape[0], 1), F32),
                                     n_groups=n_groups, epg=epg, n_valid=n_tok)
    tr = ROW_TILE
    counts = cnt[GROUP_ROWS:GROUP_ROWS + n_exp, 0].astype(I32)
    tiles = (counts + tr - 1) // tr
    tile_end = jnp.cumsum(tiles)
    row_off = (tile_end - tiles) * tr
    n_used = tile_end[-1:]
    nt_max = -(-TOP_K * n_tok // tr) + n_exp
    tile_expert = jnp.searchsorted(tile_end, jnp.minimum(jnp.arange(nt_max, dtype=I32), n_used - 1),
                                   side="right").astype(I32)
    pos = row_off[ids] + rank
    xsort = _dispatch(h, pos, jnp.zeros((nt_max * tr, D), F32), n_tok=n_tok)
    eo = _experts(xsort, w_gate, w_up, w_down, l, tile_expert, n_used, tr=tr)
    return _combine(x, eo, pos, wts.T)


def _pair_expert_kernel(pe_ref, pt_ref, pw_ref, h_ref, x_ref, wg_ref, wu_ref, wd_ref, out_ref):
    del pe_ref
    f = pl.program_id(0)
    p = pl.program_id(1)

    @pl.when((f == 0) & (p == 0))
    def _():
        out_ref[...] = x_ref[...]

    h = h_ref[...]
    g = _dot_hp(h, wg_ref[...])
    u = _dot_hp(h, wu_ref[...])
    y = _dot_hp(g / (1.0 + jnp.exp(-g)) * u, wd_ref[...])
    rows = lax.broadcasted_iota(I32, (h.shape[0], 1), 0)
    out_ref[...] += jnp.where(rows == pt_ref[p], pw_ref[p], 0.0) * y


def _moe_hp(x, l, norm, wt, bias, w_gate, w_up, w_down, *, n_groups, epg, tf=256):
    n_tok, D = x.shape
    F = w_gate.shape[3]
    tf = _pick(F, tf)
    m_pad = -(-n_tok // LANES) * LANES
    h, ids, _, wts, _ = _router(jnp.pad(x, ((0, m_pad - n_tok), (0, 0))), norm, wt, bias,
                                jnp.zeros((wt.shape[0], 1), F32),
                                n_groups=n_groups, epg=epg, n_valid=n_tok)
    pair_e = ids[:, :n_tok].reshape(-1)
    order = jnp.argsort(pair_e)
    pair_e = pair_e[order]
    pair_t = jnp.tile(jnp.arange(n_tok, dtype=I32), TOP_K)[order]
    pair_w = wts[:, :n_tok].reshape(-1)[order]
    n_pairs = TOP_K * n_tok
    full = pl.BlockSpec((n_tok, D), lambda f, p, pe, pt, pw: (0, 0))
    return pl.pallas_call(
        _pair_expert_kernel,
        out_shape=jax.ShapeDtypeStruct((n_tok, D), F32),
        grid_spec=pltpu.PrefetchScalarGridSpec(
            num_scalar_prefetch=3,
            grid=(F // tf, n_pairs),
            in_specs=[full, full,
                      pl.BlockSpec((None, None, D, tf), lambda f, p, pe, pt, pw: (l, pe[p], 0, f)),
                      pl.BlockSpec((None, None, D, tf), lambda f, p, pe, pt, pw: (l, pe[p], 0, f)),
                      pl.BlockSpec((None, None, tf, D), lambda f, p, pe, pt, pw: (l, pe[p], f, 0))],
            out_specs=full),
        compiler_params=_params(("arbitrary", "arbitrary")),
        name="pair_experts",
    )(pair_e, pair_t, pair_w, h[:n_tok], x, w_gate, w_up, w_down)


def _sb_prompt_kernel(bias_ref, q_ref, k_ref, v_ref, o_ref, *, tq, scale):
    h = pl.program_id(1)
    qi = pl.program_id(2)
    q = q_ref[...]
    bias = bias_ref[h]
    dh = q.shape[1]
    row = lax.broadcasted_iota(I32, (tq, tq), 0)
    col = lax.broadcasted_iota(I32, (tq, tq), 1)
    later = jnp.where(row > col, 1.0, 0.0).astype(BF16)
    nt = (((1,), (1,)), ((), ()))

    def step(jj, carry):
        c, acc = carry
        j = qi - jj
        ks = pl.ds(pl.multiple_of(j * tq, tq), tq)
        kt = k_ref[ks, :]
        vt = v_ref[ks, :]
        z = lax.dot_general(q, kt, nt, preferred_element_type=F32) * scale + bias
        lsig = _log_sigmoid(z)
        mask = (j * tq + col) < (qi * tq + row)
        lstay = jnp.where(mask, lsig - z, 0.0)
        hi, lo = _split_bf16(lstay)
        la = (jnp.dot(hi, later, preferred_element_type=F32)
              + jnp.dot(lo, later, preferred_element_type=F32) + c)
        a = jnp.where(mask, jnp.exp(lsig + la), 0.0)
        acc = acc + jnp.dot(a.astype(BF16), vt, preferred_element_type=F32)
        c = c + jnp.sum(lstay, axis=1, keepdims=True)
        return c, acc

    _, acc = lax.fori_loop(0, qi + 1, step,
                           (jnp.zeros((tq, 1), F32), jnp.zeros((tq, dh), F32)))
    o_ref[...] = acc.astype(o_ref.dtype)


def _sb_prompt(q, k, v, bias, *, n_heads, tq=256):
    B, T, D = q.shape
    dh = D // n_heads
    tq = _pick(T, tq)
    return pl.pallas_call(
        functools.partial(_sb_prompt_kernel, tq=tq, scale=dh ** -0.5),
        out_shape=jax.ShapeDtypeStruct((B, T, D), BF16),
        grid_spec=pltpu.PrefetchScalarGridSpec(
            num_scalar_prefetch=1,
            grid=(B, n_heads, T // tq),
            in_specs=[pl.BlockSpec((None, tq, dh), lambda b, h, i, s: (b, i, h)),
                      pl.BlockSpec((None, T, dh), lambda b, h, i, s: (b, 0, h)),
                      pl.BlockSpec((None, T, dh), lambda b, h, i, s: (b, 0, h))],
            out_specs=pl.BlockSpec((None, tq, dh), lambda b, h, i, s: (b, i, h))),
        compiler_params=_params(("parallel", "parallel", "arbitrary")),
        name="sb_prompt",
    )(bias, q, k, v)


def _sb_decode_kernel(pt_ref, q_ref, bias_ref, ck_hbm, cv_hbm, o_ref,
                      kbuf_ref, vbuf_ref, qbd_ref, sel_ref, carry_ref, acc_ref, sem,
                      *, n_heads, n_seq, n_pages, scale):
    s = pl.program_id(0)
    p = pl.program_id(1)
    page, D = kbuf_ref.shape[1:]
    dh = D // n_heads
    step = s * n_pages + p
    slot = step % 2

    def page_copies(seq, pg, sl):
        phys = pt_ref[seq * n_pages + n_pages - 1 - pg]
        cps = []
        for h in range(n_heads):
            cols = pl.ds(h * dh, dh)
            cps.append(pltpu.make_async_copy(ck_hbm.at[phys, :, h, :], kbuf_ref.at[sl, :, cols], sem.at[sl]))
            cps.append(pltpu.make_async_copy(cv_hbm.at[phys, :, h, :], vbuf_ref.at[sl, :, cols], sem.at[sl]))
        return cps

    @pl.when(step == 0)
    def _():
        for cp in page_copies(s, p, slot):
            cp.start()

    @pl.when(step + 1 < n_seq * n_pages)
    def _():
        wrap = p + 1 == n_pages
        for cp in page_copies(jnp.where(wrap, s + 1, s), jnp.where(wrap, 0, p + 1), 1 - slot):
            cp.start()

    def head_mask(n_rows):
        hrow = lax.broadcasted_iota(I32, (n_rows, D), 0)
        c = lax.broadcasted_iota(I32, (n_rows, D), 1)
        return (c >= hrow * dh) & (c < (hrow + 1) * dh)

    @pl.when(p == 0)
    def _():
        m = head_mask(LANES)
        q_hi, q_lo = _split_bf16(jnp.where(m, q_ref[...], 0.0))
        qbd_ref[0] = q_hi
        qbd_ref[1] = q_lo
        sel_ref[...] = jnp.where(m, 1.0, 0.0).astype(BF16)
        carry_ref[...] = jnp.zeros_like(carry_ref)
        acc_ref[...] = jnp.zeros_like(acc_ref)

    for cp in page_copies(s, p, slot):
        cp.wait()
    kcat = kbuf_ref[slot]
    k_hi = kcat.astype(BF16)
    k_lo = (kcat - k_hi.astype(F32)).astype(BF16)
    nt = (((1,), (1,)), ((), ()))
    kk = jnp.concatenate([k_hi, k_lo], axis=0)
    zz = lax.dot_general(kk, qbd_ref[0], nt, preferred_element_type=F32)
    z = zz[:page] + zz[page:] + lax.dot_general(k_hi, qbd_ref[1], nt, preferred_element_type=F32)
    z = z * scale + bias_ref[...]
    lsig = _log_sigmoid(z)
    lstay = lsig - z
    row = lax.broadcasted_iota(I32, (page, page), 0)
    col = lax.broadcasted_iota(I32, (page, page), 1)
    later = jnp.where(col > row, 1.0, 0.0).astype(BF16)
    hi, lo = _split_bf16(lstay)
    la = (jnp.dot(later, hi, preferred_element_type=F32)
          + jnp.dot(later, lo, preferred_element_type=F32) + carry_ref[...])
    a = jnp.exp(lsig + la)
    carry_ref[...] += jnp.sum(lstay, axis=0, keepdims=True)
    a_hi, a_lo = _split_bf16(a)
    ae = jnp.dot(jnp.concatenate([a_hi, a_lo], axis=0), sel_ref[...], preferred_element_type=F32)
    prod = (ae[:page] + ae[page:]) * vbuf_ref[slot]
    acc_ref[...] += jnp.sum(prod.reshape(page // SUBLANES, SUBLANES, D), axis=0)

    @pl.when(p == n_pages - 1)
    def _():
        o_ref[...] = jnp.sum(acc_ref[...], axis=0, keepdims=True)


def _sb_decode(q, cache_k, cache_v, page_table, bias):
    n_seq, D = q.shape
    _, page, n_heads, dh = cache_k.shape
    n_pages = page_table.shape[1]
    assert n_heads <= LANES and page % BF16_ROWS == 0
    bias_row = jnp.zeros((1, LANES), F32).at[0, :n_heads].set(bias)

    out = pl.pallas_call(
        functools.partial(_sb_decode_kernel, n_heads=n_heads, n_seq=n_seq, n_pages=n_pages,
                          scale=dh ** -0.5),
        out_shape=jax.ShapeDtypeStruct((n_seq, 1, D), F32),
        grid_spec=pltpu.PrefetchScalarGridSpec(
            num_scalar_prefetch=1,
            grid=(n_seq, n_pages),
            in_specs=[pl.BlockSpec((None, 1, D), lambda s, p, pt: (s, 0, 0)),
                      pl.BlockSpec((1, LANES), lambda s, p, pt: (0, 0)),
                      pl.BlockSpec(memory_space=pl.ANY), pl.BlockSpec(memory_space=pl.ANY)],
            out_specs=pl.BlockSpec((None, 1, D), lambda s, p, pt: (s, 0, 0)),
            scratch_shapes=[pltpu.VMEM((2, page, D), F32), pltpu.VMEM((2, page, D), F32),
                            pltpu.VMEM((2, LANES, D), BF16), pltpu.VMEM((LANES, D), BF16),
                            pltpu.VMEM((1, LANES), F32), pltpu.VMEM((SUBLANES, D), F32),
                            pltpu.SemaphoreType.DMA((2,))]),
        compiler_params=_params(("arbitrary", "arbitrary")),
        name="sb_decode",
    )(page_table.reshape(-1), q.reshape(n_seq, 1, D), bias_row, cache_k, cache_v)
    return out.reshape(n_seq, D)


def _rmsnorm_kernel(x_ref, g_ref, o_ref):
    o_ref[...] = _rms(x_ref[...], g_ref[...])


def _rmsnorm(x, gamma, *, tm=1024):
    M, D = x.shape
    tm = _pick(M, tm)
    return pl.pallas_call(
        _rmsnorm_kernel,
        out_shape=jax.ShapeDtypeStruct((M, D), F32),
        grid=(M // tm,),
        in_specs=[pl.BlockSpec((tm, D), lambda i: (i, 0)), pl.BlockSpec((1, D), lambda i: (0, 0))],
        out_specs=pl.BlockSpec((tm, D), lambda i: (i, 0)),
        compiler_params=_params(("parallel",)),
        name="rmsnorm",
    )(x, gamma.reshape(1, D))


def kernel(x_prompt, x_sample, state_conv, cache_k, cache_v, page_table, norm_mix, norm_ffn,
           conv_w_in, conv_w, conv_w_out, kv_norm, w_kv, attn_w_q, attn_b, attn_w_o,
           router_group_w, router_group_b, router_expert_w, router_expert_b,
           expert_w_gate, expert_w_up, expert_w_down, final_norm):
    bp, tp, D = x_prompt.shape
    bs, ts, _ = x_sample.shape
    assert ts == 1
    depth = norm_mix.shape[0]
    n_a = conv_w_in.shape[0]
    n_heads = cache_k.shape[2]
    n_groups = router_group_w.shape[2]
    n_exp = router_expert_w.shape[2]
    epg = n_exp // n_groups

    xp = x_prompt.reshape(bp * tp, D)
    xs = x_sample.reshape(bs, D)
    n_r = GROUP_ROWS + n_exp
    wt = jnp.zeros((depth, n_r, D), F32)
    wt = wt.at[:, :n_groups].set(jnp.swapaxes(router_group_w, 1, 2))
    wt = wt.at[:, GROUP_ROWS:].set(jnp.swapaxes(router_expert_w, 1, 2))
    rb = jnp.zeros((depth, n_r, 1), F32)
    rb = rb.at[:, :n_groups, 0].set(router_group_b)
    rb = rb.at[:, GROUP_ROWS:, 0].set(router_expert_b)
    w_kv3 = w_kv[None]

    n_hist = conv_w.shape[1] - 1
    conv_p, conv_s = [], []
    kp = vp = kp_b = vp_b = ks_new = vs_new = None
    for l in range(depth):
        if l < n_a:
            gp, stp = _conv_in(xp, norm_mix[l], conv_w_in, l, conv_w[l], seq_len=tp)
            xp = _linear(gp, conv_w_out, l, res=xp)
            if l == 0 and n_a > 1 and tp >= 2 * n_hist:
                x_win = x_prompt[:, tp - 2 * n_hist:, :]
                bcx_win = _linear(x_win.reshape(bp * 2 * n_hist, D), conv_w_in, l, gamma=norm_mix[l],
                                  hp=True).reshape(bp, 2 * n_hist, 3 * D)
                rows_of = lambda a, i: jnp.concatenate([a[:, j + i] for j in range(n_hist)], axis=0)
                g_tail, _ = _conv_step(rows_of(bcx_win, n_hist), [rows_of(bcx_win, i) for i in range(n_hist)],
                                       conv_w[l], hist_is_bcx=True)
                x_tail = _linear(g_tail, conv_w_out, l, res=rows_of(x_win, n_hist), hp=True)
                tail_rows = jnp.concatenate([jnp.arange(bp, dtype=I32) * tp + tp - n_hist + j
                                             for j in range(n_hist)])
                xp = xp.at[tail_rows].set(x_tail)
            bcx = _linear(xs, conv_w_in, l, gamma=norm_mix[l], hp=True)
            gs, sts = _conv_step(bcx, [state_conv[l][:, i] for i in range(n_hist)], conv_w[l],
                                 hist_is_bcx=False)
            xs = _linear(gs, conv_w_out, l, res=xs, hp=True)
            conv_p.append(stp)
            conv_s.append(sts)
        else:
            j = l - n_a
            qp = _linear(xp, attn_w_q, j, gamma=norm_mix[l], out_dtypes=(BF16,))
            ap = _sb_prompt(qp.reshape(bp, tp, D), kp_b.reshape(bp, tp, D), vp_b.reshape(bp, tp, D),
                            attn_b[j], n_heads=n_heads)
            xp = _linear(ap.reshape(bp * tp, D), attn_w_o, j, res=xp)
            qs = _linear(xs, attn_w_q, j, gamma=norm_mix[l], hp=True)
            a_s = _sb_decode(qs, cache_k, cache_v, page_table, attn_b[j])
            xs = _linear(a_s, attn_w_o, j, res=xs, hp=True)
        moe_args = (l, norm_ffn[l], wt[l], rb[l], expert_w_gate, expert_w_up, expert_w_down)
        xp = _moe(xp, *moe_args, n_groups=n_groups, epg=epg)
        xs = _moe_hp(xs, *moe_args, n_groups=n_groups, epg=epg)
        if l == n_a - 1:
            kp, kp_b = _linear(xp, w_kv3, 0, gamma=kv_norm, col_off=0, n_cols=D, out_dtypes=(F32, BF16))
            vp, vp_b = _linear(xp, w_kv3, 0, gamma=kv_norm, col_off=D, n_cols=D, out_dtypes=(F32, BF16))
            ks_new = _linear(xs, w_kv3, 0, gamma=kv_norm, col_off=0, n_cols=D, hp=True)
            vs_new = _linear(xs, w_kv3, 0, gamma=kv_norm, col_off=D, n_cols=D, hp=True)
    dh = D // n_heads
    y_prompt = _rmsnorm(xp, final_norm).reshape(bp, tp, D)
    y_sample = _rmsnorm(xs, final_norm).reshape(bs, ts, D)
    return (y_prompt, y_sample, jnp.stack(conv_p, axis=0), jnp.stack(conv_s, axis=0),
            kp.reshape(bp, tp, n_heads, dh), vp.reshape(bp, tp, n_heads, dh),
            ks_new.reshape(bs, ts, n_heads, dh), vs_new.reshape(bs, ts, n_heads, dh))
```

```python
import functools

import jax
import jax.numpy as jnp
from jax import lax
from jax.experimental import pallas as pl
from jax.experimental.pallas import tpu as pltpu

F32 = jnp.float32
BF16 = jnp.bfloat16
I32 = jnp.int32

RMS_EPS = 1e-6
TOP_K = 2

LANES = 128
SUBLANES = 8
BF16_ROWS = 16
VMEM_LIMIT = 52 * 1024 * 1024

ROW_TILE = 256
GROUP_ROWS = SUBLANES


def _pick(n, pref):
    t = min(n, pref)
    while n % t:
        t //= 2
    return t


def _params(sem):
    return pltpu.CompilerParams(dimension_semantics=sem, vmem_limit_bytes=VMEM_LIMIT)


def _rms(x, g):
    var = jnp.mean(x * x, axis=-1, keepdims=True)
    return x * lax.rsqrt(var + RMS_EPS) * g


def _split_bf16(x):
    hi = x.astype(BF16)
    lo = (x - hi.astype(F32)).astype(BF16)
    return hi, lo


def _log_sigmoid_pair(z, *, accurate_tail):
    u = jnp.exp(-jnp.abs(z))
    t = jnp.log1p(u) if accurate_tail else jnp.log(1.0 + u)
    return jnp.minimum(z, 0.0) - t, -jnp.maximum(z, 0.0) - t


def _dot_hp(x, w):
    m = x.shape[0]
    assert m % SUBLANES == 0
    x_hi = x.astype(BF16).astype(F32)
    xx = jnp.concatenate([x_hi, x - x_hi], axis=0).astype(BF16)
    w_hi, w_lo = _split_bf16(w)
    r = (jnp.dot(xx, w_hi, preferred_element_type=F32)
         + jnp.dot(xx, w_lo, preferred_element_type=F32))
    return r[:m] + r[m:]


def _linear_kernel(*refs, has_norm, has_res, n_out, hp):
    refs = list(refs)
    x_ref = refs.pop(0)
    g_ref = refs.pop(0) if has_norm else None
    w_ref = refs.pop(0)
    r_ref = refs.pop(0) if has_res else None
    out_refs = refs[:n_out]
    if has_norm:
        h_ref = refs[n_out]

        @pl.when(pl.program_id(1) == 0)
        def _():
            h_ref[...] = _rms(x_ref[...], g_ref[...]).astype(h_ref.dtype)

        h = h_ref[...]
    else:
        h = x_ref[...]
    if hp:
        acc = _dot_hp(h.astype(F32), w_ref[...])
    else:
        acc = jnp.dot(h.astype(BF16), w_ref[...].astype(BF16), preferred_element_type=F32)
    if has_res:
        acc = acc + r_ref[...]
    for o in out_refs:
        o[...] = acc.astype(o.dtype)


def _linear(x, w3, l, *, gamma=None, res=None, col_off=0, n_cols=None,
            out_dtypes=(F32,), tm=1024, tn=512, hp=False):
    m_rows, K = x.shape
    if hp and m_rows % SUBLANES:
        pad = ((0, SUBLANES - m_rows % SUBLANES), (0, 0))
        x = jnp.pad(x, pad)
        res = None if res is None else jnp.pad(res, pad)
    M = x.shape[0]
    n_cols = n_cols or w3.shape[2]
    tm = _pick(M, tm)
    tn = _pick(n_cols, tn)
    assert col_off % tn == 0
    jo = col_off // tn
    in_specs = [pl.BlockSpec((tm, K), lambda i, j: (i, 0))]
    args = [x]
    if gamma is not None:
        in_specs.append(pl.BlockSpec((1, K), lambda i, j: (0, 0)))
        args.append(gamma.reshape(1, K))
    in_specs.append(pl.BlockSpec((None, K, tn), lambda i, j: (l, 0, j + jo)))
    args.append(w3)
    if res is not None:
        in_specs.append(pl.BlockSpec((tm, tn), lambda i, j: (i, j)))
        args.append(res)
    scratch = []
    if gamma is not None:
        scratch.append(pltpu.VMEM((tm, K), F32 if hp or tm % BF16_ROWS else BF16))
    outs = pl.pallas_call(
        functools.partial(_linear_kernel, has_norm=gamma is not None,
                          has_res=res is not None, n_out=len(out_dtypes), hp=hp),
        out_shape=[jax.ShapeDtypeStruct((M, n_cols), dt) for dt in out_dtypes],
        grid=(M // tm, n_cols // tn),
        in_specs=in_specs,
        out_specs=[pl.BlockSpec((tm, tn), lambda i, j: (i, j)) for _ in out_dtypes],
        scratch_shapes=scratch,
        compiler_params=_params(("parallel", "arbitrary")),
        name="linear",
    )(*args)
    outs = [o[:m_rows] for o in outs] if M != m_rows else outs
    return outs[0] if len(out_dtypes) == 1 else outs


def _conv_in_kernel(x_ref, g_ref, wb_ref, wc_ref, wx_ref, cw_ref, gout_ref, st_ref,
                    h_ref, carry_ref, *, tiles_per_seq, tn, conv_w):
    i = pl.program_id(0)
    j = pl.program_id(1)

    @pl.when(j == 0)
    def _():
        h_ref[...] = _rms(x_ref[...], g_ref[...]).astype(BF16)

    h = h_ref[...]
    b = jnp.dot(h, wb_ref[...].astype(BF16), preferred_element_type=F32)
    c = jnp.dot(h, wc_ref[...].astype(BF16), preferred_element_type=F32)
    xin = jnp.dot(h, wx_ref[...].astype(BF16), preferred_element_type=F32)
    u = c * xin
    tm = u.shape[0]
    cols = pl.ds(pl.multiple_of(j * tn, tn), tn)
    first = (i % tiles_per_seq) == 0
    prev = jnp.where(first, 0.0, carry_ref[:, cols])
    cw = cw_ref[...]
    row8 = lax.broadcasted_iota(I32, (SUBLANES, tn), 0)
    conv_top = cw[conv_w - 1:conv_w, :] * u[:SUBLANES]
    conv_rest = cw[conv_w - 1:conv_w, :] * u[SUBLANES:]
    for s in range(1, conv_w):
        wrow = cw[conv_w - 1 - s:conv_w - s, :]
        shifted = pltpu.roll(u, s, 0)
        top = jnp.where(row8 < s, pltpu.roll(prev, s, 0), shifted[:SUBLANES])
        conv_top = conv_top + wrow * top
        conv_rest = conv_rest + wrow * shifted[SUBLANES:]
    conv = jnp.concatenate([conv_top, conv_rest], axis=0)
    gout_ref[...] = (b * conv).astype(gout_ref.dtype)
    last8 = u[tm - SUBLANES:, :]
    carry_ref[:, cols] = last8
    st_ref[...] = last8


def _conv_in(x, gamma, w_in3, l, cw, *, seq_len, tm=1024, tn=256):
    M, D = x.shape
    conv_w = cw.shape[0]
    tm = _pick(seq_len, tm)
    tn = _pick(D, tn)
    nD = D // tn
    assert conv_w - 1 <= SUBLANES and tm >= 2 * SUBLANES
    tiles_per_seq = seq_len // tm
    g, st = pl.pallas_call(
        functools.partial(_conv_in_kernel, tiles_per_seq=tiles_per_seq, tn=tn, conv_w=conv_w),
        out_shape=[jax.ShapeDtypeStruct((M, D), BF16),
                   jax.ShapeDtypeStruct((M // tm, SUBLANES, D), F32)],
        grid=(M // tm, nD),
        in_specs=[pl.BlockSpec((tm, D), lambda i, j: (i, 0)),
                  pl.BlockSpec((1, D), lambda i, j: (0, 0)),
                  pl.BlockSpec((None, D, tn), lambda i, j: (l, 0, j)),
                  pl.BlockSpec((None, D, tn), lambda i, j: (l, 0, j + nD)),
                  pl.BlockSpec((None, D, tn), lambda i, j: (l, 0, j + 2 * nD)),
                  pl.BlockSpec((conv_w, tn), lambda i, j: (0, j))],
        out_specs=[pl.BlockSpec((tm, tn), lambda i, j: (i, j)),
                   pl.BlockSpec((None, SUBLANES, tn), lambda i, j: (i, 0, j))],
        scratch_shapes=[pltpu.VMEM((tm, D), BF16), pltpu.VMEM((SUBLANES, D), F32)],
        compiler_params=_params(("arbitrary", "arbitrary")),
        name="conv_in",
    )(x, gamma.reshape(1, D), w_in3, w_in3, w_in3, cw)
    return g, st[tiles_per_seq - 1::tiles_per_seq, SUBLANES - (conv_w - 1):, :]


def _conv_step_kernel(bcx_ref, *refs, d, conv_w, hist_is_bcx):
    hist_refs = refs[:conv_w - 1]
    cw_ref, g_ref, nst_ref = refs[conv_w - 1:]

    def conv_input(ref):
        return ref[:, d:2 * d] * ref[:, 2 * d:3 * d]

    b = bcx_ref[:, 0:d]
    u_ext = [conv_input(r) if hist_is_bcx else r[...] for r in hist_refs] + [conv_input(bcx_ref)]
    conv = cw_ref[0:1, :] * u_ext[0]
    for i in range(1, conv_w):
        conv = conv + cw_ref[i:i + 1, :] * u_ext[i]
    g_ref[...] = b * conv
    for i in range(conv_w - 1):
        nst_ref[:, i * d:(i + 1) * d] = u_ext[i + 1]


def _conv_step(bcx, hist, cw, *, hist_is_bcx):
    n = bcx.shape[0]
    conv_w, d = cw.shape
    g, nst = pl.pallas_call(
        functools.partial(_conv_step_kernel, d=d, conv_w=conv_w, hist_is_bcx=hist_is_bcx),
        out_shape=[jax.ShapeDtypeStruct((n, d), F32),
                   jax.ShapeDtypeStruct((n, (conv_w - 1) * d), F32)],
        name="conv_step",
    )(bcx, *hist, cw)
    return g, nst.reshape(n, conv_w - 1, d)


def _router_kernel(x_ref, g_ref, wt_ref, bias_ref, base_ref, ut_ref,
                   h_ref, ids_ref, rank_ref, wts_ref, cnt_ref,
                   *, n_groups, epg, n_valid, tm):
    i = pl.program_id(0)

    @pl.when(i == 0)
    def _():
        cnt_ref[...] = base_ref[...]

    hn = _rms(x_ref[...], g_ref[...])
    h_ref[...] = hn
    h_hi, h_lo = _split_bf16(hn)
    w_hi, w_lo = _split_bf16(wt_ref[...])
    nt = (((1,), (1,)), ((), ()))
    logits = (lax.dot_general(w_hi, h_hi, nt, preferred_element_type=F32)
              + lax.dot_general(w_hi, h_lo, nt, preferred_element_type=F32)
              + lax.dot_general(w_lo, h_hi, nt, preferred_element_type=F32))
    logits = logits + bias_ref[...]
    n_rows = logits.shape[0]
    rows = lax.broadcasted_iota(I32, (n_rows, tm), 0)
    big = jnp.int32(n_rows)
    neg = jnp.float32(-jnp.inf)

    is_g = rows < n_groups
    lg = jnp.where(is_g, logits, neg)
    eg = jnp.exp(lg - jnp.max(lg, axis=0, keepdims=True))
    pg = eg / jnp.sum(eg, axis=0, keepdims=True)
    g_val = jnp.max(pg, axis=0, keepdims=True)
    g_idx = jnp.min(jnp.where((pg == g_val) & is_g, rows, big), axis=0, keepdims=True)

    lo_row = GROUP_ROWS + g_idx * epg
    in_grp = (rows >= lo_row) & (rows < lo_row + epg)
    le = jnp.where(in_grp, logits, neg)
    ee = jnp.exp(le - jnp.max(le, axis=0, keepdims=True))
    pe = jnp.where(in_grp, ee / jnp.sum(ee, axis=0, keepdims=True), -1.0)
    v1 = jnp.max(pe, axis=0, keepdims=True)
    i1 = jnp.min(jnp.where(pe == v1, rows, big), axis=0, keepdims=True)
    pe2 = jnp.where(rows == i1, -1.0, pe)
    v2 = jnp.max(pe2, axis=0, keepdims=True)
    i2 = jnp.min(jnp.where(pe2 == v2, rows, big), axis=0, keepdims=True)
    den = v1 + v2
    wts_ref[0:1, :] = g_val * (v1 / den)
    wts_ref[1:2, :] = g_val * (v2 / den)
    ids_ref[0:1, :] = i1 - GROUP_ROWS
    ids_ref[1:2, :] = i2 - GROUP_ROWS

    tok = i * tm + lax.broadcasted_iota(I32, (n_rows, tm), 1)
    sel1 = rows == i1
    sel2 = rows == i2
    oh = jnp.where((sel1 | sel2) & (tok < n_valid), 1.0, 0.0)
    cum = jnp.dot(oh.astype(BF16), ut_ref[...], preferred_element_type=F32) + cnt_ref[...]
    rank_ref[0:1, :] = (jnp.sum(jnp.where(sel1, cum, 0.0), axis=0, keepdims=True) - 1.0).astype(I32)
    rank_ref[1:2, :] = (jnp.sum(jnp.where(sel2, cum, 0.0), axis=0, keepdims=True) - 1.0).astype(I32)
    cnt_ref[...] += jnp.sum(oh, axis=1, keepdims=True)


def _router(x, gamma, wt, bias, base, *, n_groups, epg, n_valid, tm=512):
    M, D = x.shape
    tm = _pick(M, tm)
    R = wt.shape[0]
    ut = jnp.triu(jnp.ones((tm, tm), BF16))
    full = lambda shape: pl.BlockSpec(shape, lambda i: (0,) * len(shape))
    tokspec = lambda dt: (jax.ShapeDtypeStruct((TOP_K, M), dt), pl.BlockSpec((TOP_K, tm), lambda i: (0, i)))
    (ids_s, ids_b), (rank_s, rank_b), (wts_s, wts_b) = tokspec(I32), tokspec(I32), tokspec(F32)
    return pl.pallas_call(
        functools.partial(_router_kernel, n_groups=n_groups, epg=epg, n_valid=n_valid, tm=tm),
        out_shape=[jax.ShapeDtypeStruct((M, D), F32), ids_s, rank_s, wts_s,
                   jax.ShapeDtypeStruct((R, 1), F32)],
        grid=(M // tm,),
        in_specs=[pl.BlockSpec((tm, D), lambda i: (i, 0)), full((1, D)), full((R, D)),
                  full((R, 1)), full((R, 1)), full((tm, tm))],
        out_specs=[pl.BlockSpec((tm, D), lambda i: (i, 0)), ids_b, rank_b, wts_b, full((R, 1))],
        compiler_params=_params(("arbitrary",)),
        name="router",
    )(x, gamma.reshape(1, D), wt, bias, base, ut)


def _dispatch_kernel(pos_ref, h_ref, xs_in_hbm, xs_hbm, sem, *, tc, n_tok):
    del xs_in_hbm
    base = pl.program_id(0) * tc

    def row_copy(t, p):
        return pltpu.make_async_copy(h_ref.at[pl.ds(t, 1), :], xs_hbm.at[pl.ds(p, 1), :], sem)

    def issue(t, c):
        for k in range(TOP_K):
            row_copy(t, pos_ref[k * n_tok + base + t]).start()
        return c

    lax.fori_loop(0, tc, issue, 0, unroll=8)

    def drain(t, c):
        row_copy(0, 0).wait()
        return c

    lax.fori_loop(0, TOP_K * tc, drain, 0, unroll=8)


def _dispatch(h, pos, xs, *, tc=512):
    n_tok, D = h.shape
    tc = _pick(n_tok, tc)
    return pl.pallas_call(
        functools.partial(_dispatch_kernel, tc=tc, n_tok=n_tok),
        out_shape=jax.ShapeDtypeStruct(xs.shape, xs.dtype),
        grid_spec=pltpu.PrefetchScalarGridSpec(
            num_scalar_prefetch=1,
            grid=(n_tok // tc,),
            in_specs=[pl.BlockSpec((tc, D), lambda i, p: (i, 0)), pl.BlockSpec(memory_space=pl.ANY)],
            out_specs=pl.BlockSpec(memory_space=pl.ANY),
            scratch_shapes=[pltpu.SemaphoreType.DMA(())]),
        input_output_aliases={2: 0},
        compiler_params=_params(("arbitrary",)),
        name="dispatch",
    )(pos.reshape(-1), h, xs)


def _gate_up_kernel(te_ref, nu_ref, xs_ref, wg_ref, wu_ref, act_ref, wgb_ref, wub_ref):
    r = pl.program_id(1)
    used = r < nu_ref[0]
    new_expert = (r == 0) | (te_ref[r] != te_ref[jnp.maximum(r - 1, 0)])

    @pl.when(used & new_expert)
    def _():
        wgb_ref[...] = wg_ref[...].astype(BF16)
        wub_ref[...] = wu_ref[...].astype(BF16)

    @pl.when(used)
    def _():
        x = xs_ref[...].astype(BF16)
        g = jnp.dot(x, wgb_ref[...], preferred_element_type=F32)
        u = jnp.dot(x, wub_ref[...], preferred_element_type=F32)
        act_ref[...] = (g / (1.0 + jnp.exp(-g)) * u).astype(act_ref.dtype)

    @pl.when(jnp.logical_not(used))
    def _():
        act_ref[...] = jnp.zeros_like(act_ref)


def _down_kernel(te_ref, nu_ref, act_ref, wd_ref, out_ref, wdb_ref):
    r = pl.program_id(1)
    used = r < nu_ref[0]
    new_expert = (r == 0) | (te_ref[r] != te_ref[jnp.maximum(r - 1, 0)])

    @pl.when(used & new_expert)
    def _():
        wdb_ref[...] = wd_ref[...].astype(BF16)

    @pl.when(used)
    def _():
        out_ref[...] = jnp.dot(act_ref[...], wdb_ref[...], preferred_element_type=F32)

    @pl.when(jnp.logical_not(used))
    def _():
        out_ref[...] = jnp.zeros_like(out_ref)


def _experts(xs, w_gate, w_up, w_down, l, tile_expert, n_used, *, tr, tf=512, tn=1024):
    R, D = xs.shape
    F = w_gate.shape[3]
    tf = _pick(F, tf)
    tn = _pick(D, tn)
    nt = R // tr
    row = lambda f, r, te, nu: jnp.minimum(r, nu[0] - 1)
    act = pl.pallas_call(
        _gate_up_kernel,
        out_shape=jax.ShapeDtypeStruct((R, F), BF16),
        grid_spec=pltpu.PrefetchScalarGridSpec(
            num_scalar_prefetch=2,
            grid=(F // tf, nt),
            in_specs=[pl.BlockSpec((tr, D), lambda f, r, te, nu: (row(f, r, te, nu), 0)),
                      pl.BlockSpec((None, None, D, tf), lambda f, r, te, nu: (l, te[r], 0, f)),
                      pl.BlockSpec((None, None, D, tf), lambda f, r, te, nu: (l, te[r], 0, f))],
            out_specs=pl.BlockSpec((tr, tf), lambda f, r, te, nu: (r, f)),
            scratch_shapes=[pltpu.VMEM((D, tf), BF16), pltpu.VMEM((D, tf), BF16)]),
        compiler_params=_params(("arbitrary", "arbitrary")),
        name="expert_gate_up",
    )(tile_expert, n_used, xs, w_gate, w_up)
    return pl.pallas_call(
        _down_kernel,
        out_shape=jax.ShapeDtypeStruct((R, D), F32),
        grid_spec=pltpu.PrefetchScalarGridSpec(
            num_scalar_prefetch=2,
            grid=(D // tn, nt),
            in_specs=[pl.BlockSpec((tr, F), lambda n, r, te, nu: (row(n, r, te, nu), 0)),
                      pl.BlockSpec((None, None, F, tn), lambda n, r, te, nu: (l, te[r], 0, n))],
            out_specs=pl.BlockSpec((tr, tn), lambda n, r, te, nu: (r, n)),
            scratch_shapes=[pltpu.VMEM((F, tn), BF16)]),
        compiler_params=_params(("arbitrary", "arbitrary")),
        name="expert_down",
    )(tile_expert, n_used, act, w_down)


def _combine_kernel(pos_ref, x_ref, w_ref, eo_hbm, out_ref, buf_ref, sem, *, tm, m_pad):
    base = pl.program_id(0) * tm

    def row_copy(k, t, p):
        return pltpu.make_async_copy(eo_hbm.at[pl.ds(p, 1), :], buf_ref.at[k, pl.ds(t, 1), :], sem)

    def issue(t, c):
        for k in range(TOP_K):
            row_copy(k, t, pos_ref[k * m_pad + base + t]).start()
        return c

    lax.fori_loop(0, tm, issue, 0, unroll=8)

    def drain(t, c):
        row_copy(0, 0, 0).wait()
        return c

    lax.fori_loop(0, TOP_K * tm, drain, 0, unroll=8)
    w = w_ref[...]
    acc = x_ref[...]
    for k in range(TOP_K):
        acc = acc + w[:, k:k + 1] * buf_ref[k]
    out_ref[...] = acc


def _combine(x, eo, pos, wts, *, tm=256):
    M, D = x.shape
    m_pad = pos.shape[1]
    tm = _pick(M, tm)
    return pl.pallas_call(
        functools.partial(_combine_kernel, tm=tm, m_pad=m_pad),
        out_shape=jax.ShapeDtypeStruct((M, D), F32),
        grid_spec=pltpu.PrefetchScalarGridSpec(
            num_scalar_prefetch=1,
            grid=(M // tm,),
            in_specs=[pl.BlockSpec((tm, D), lambda i, p: (i, 0)),
                      pl.BlockSpec((tm, TOP_K), lambda i, p: (i, 0)),
                      pl.BlockSpec(memory_space=pl.ANY)],
            out_specs=pl.BlockSpec((tm, D), lambda i, p: (i, 0)),
            scratch_shapes=[pltpu.VMEM((TOP_K, tm, D), F32), pltpu.SemaphoreType.DMA(())]),
        compiler_params=_params(("arbitrary",)),
        name="combine",
    )(pos.reshape(-1), x, wts, eo)


def _moe(x, l, norm, wt, bias, w_gate, w_up, w_down, *, n_groups, epg):
    n_tok, D = x.shape
    n_exp = n_groups * epg
    h, ids, rank, wts, cnt = _router(x, norm, wt, bias, jnp.zeros((wt.shape[0], 1), F32),
                                     n_groups=n_groups, epg=epg, n_valid=n_tok)
    tr = ROW_TILE
    counts = cnt[GROUP_ROWS:GROUP_ROWS + n_exp, 0].astype(I32)
    tiles = (counts + tr - 1) // tr
    tile_end = jnp.cumsum(tiles)
    row_off = (tile_end - tiles) * tr
    n_used = tile_end[-1:]
    nt_max = -(-TOP_K * n_tok // tr) + n_exp
    tile_id = jnp.minimum(jnp.arange(nt_max, dtype=I32), n_used - 1)
    tile_expert = jnp.sum((tile_end[None, :] <= tile_id[:, None]).astype(I32), axis=1)
    expert_iota = jnp.arange(n_exp, dtype=I32)[:, None, None]
    pos = jnp.sum(jnp.where(ids[None] == expert_iota, row_off[:, None, None], 0), axis=0) + rank
    xsort = _dispatch(h, pos, jnp.zeros((nt_max * tr, D), F32))
    eo = _experts(xsort, w_gate, w_up, w_down, l, tile_expert, n_used, tr=tr)
    return _combine(x, eo, pos, wts.T)


def _pair_expert_kernel(pe_ref, pt_ref, pw_ref, h_ref, x_ref, wg_ref, wu_ref, wd_ref, out_ref):
    del pe_ref
    f = pl.program_id(0)
    p = pl.program_id(1)

    @pl.when((f == 0) & (p == 0))
    def _():
        out_ref[...] = x_ref[...]

    h = h_ref[...]
    g = _dot_hp(h, wg_ref[...])
    u = _dot_hp(h, wu_ref[...])
    y = _dot_hp(g / (1.0 + jnp.exp(-g)) * u, wd_ref[...])
    rows = lax.broadcasted_iota(I32, (h.shape[0], 1), 0)
    out_ref[...] += jnp.where(rows == pt_ref[p], pw_ref[p], 0.0) * y


def _moe_hp(x, l, norm, wt, bias, w_gate, w_up, w_down, *, n_groups, epg, tf=256):
    n_tok, D = x.shape
    F = w_gate.shape[3]
    tf = _pick(F, tf)
    m_pad = -(-n_tok // LANES) * LANES
    h, ids, _, wts, _ = _router(jnp.pad(x, ((0, m_pad - n_tok), (0, 0))), norm, wt, bias,
                                jnp.zeros((wt.shape[0], 1), F32),
                                n_groups=n_groups, epg=epg, n_valid=n_tok)
    n_pairs = TOP_K * n_tok
    pair_e = ids[:, :n_tok].reshape(-1)
    pair_t = jnp.tile(jnp.arange(n_tok, dtype=I32), TOP_K)
    pair_w = wts[:, :n_tok].reshape(-1)
    pair_iota = jnp.arange(n_pairs, dtype=I32)
    key = pair_e * n_pairs + pair_iota
    slot = jnp.sum((key[None, :] < key[:, None]).astype(I32), axis=1)
    place = lambda v: jnp.sum(jnp.where(slot[None, :] == pair_iota[:, None], v[None, :], 0), axis=1)
    pair_e, pair_t, pair_w = place(pair_e), place(pair_t), place(pair_w)
    full = pl.BlockSpec((n_tok, D), lambda f, p, pe, pt, pw: (0, 0))
    return pl.pallas_call(
        _pair_expert_kernel,
        out_shape=jax.ShapeDtypeStruct((n_tok, D), F32),
        grid_spec=pltpu.PrefetchScalarGridSpec(
            num_scalar_prefetch=3,
            grid=(F // tf, n_pairs),
            in_specs=[full, full,
                      pl.BlockSpec((None, None, D, tf), lambda f, p, pe, pt, pw: (l, pe[p], 0, f)),
                      pl.BlockSpec((None, None, D, tf), lambda f, p, pe, pt, pw: (l, pe[p], 0, f)),
                      pl.BlockSpec((None, None, tf, D), lambda f, p, pe, pt, pw: (l, pe[p], f, 0))],
            out_specs=full),
        compiler_params=_params(("arbitrary", "arbitrary")),
        name="pair_experts",
    )(pair_e, pair_t, pair_w, h[:n_tok], x, w_gate, w_up, w_down)


def _sb_prompt_kernel(bias_ref, q_ref, k_ref, v_ref, o_ref, *, tq, dh, heads, scale):
    hg = pl.program_id(1)
    qi = pl.program_id(2)
    row = lax.broadcasted_iota(I32, (tq, tq), 0)
    col = lax.broadcasted_iota(I32, (tq, tq), 1)
    later = jnp.where(row > col, 1.0, 0.0).astype(BF16)
    diag = col < row
    nt = (((1,), (1,)), ((), ()))
    lanes = [slice(i * dh, (i + 1) * dh) for i in range(heads)]
    qs = [q_ref[:, ln] for ln in lanes]
    biases = [bias_ref[hg * heads + i] for i in range(heads)]

    def tile(i, j, c, acc, mask):
        ks = pl.ds(pl.multiple_of(j * tq, tq), tq)
        z = lax.dot_general(qs[i], k_ref[ks, lanes[i]], nt, preferred_element_type=F32) * scale + biases[i]
        lsig, lstay = _log_sigmoid_pair(z, accurate_tail=False)
        if mask is not None:
            lstay = jnp.where(mask, lstay, 0.0)
        hi, lo = _split_bf16(lstay)
        la = (jnp.dot(hi, later, preferred_element_type=F32)
              + jnp.dot(lo, later, preferred_element_type=F32) + c)
        a = jnp.exp(lsig + la)
        if mask is not None:
            a = jnp.where(mask, a, 0.0)
        acc = acc + jnp.dot(a.astype(BF16), v_ref[ks, lanes[i]], preferred_element_type=F32)
        return c + jnp.sum(lstay, axis=1, keepdims=True), acc

    init = tuple(tile(i, qi, jnp.zeros((tq, 1), F32), jnp.zeros((tq, dh), F32), diag)
                 for i in range(heads))

    def step(jj, carry):
        return tuple(tile(i, qi - jj, *carry[i], None) for i in range(heads))

    out = lax.fori_loop(1, qi + 1, step, init)
    for i in range(heads):
        o_ref[:, lanes[i]] = out[i][1].astype(o_ref.dtype)


def _sb_prompt(q, k, v, bias, *, n_heads, tq=256):
    B, T, D = q.shape
    dh = D // n_heads
    tq = _pick(T, tq)
    heads = 2 if n_heads % 2 == 0 else 1
    w = heads * dh
    return pl.pallas_call(
        functools.partial(_sb_prompt_kernel, tq=tq, dh=dh, heads=heads, scale=dh ** -0.5),
        out_shape=jax.ShapeDtypeStruct((B, T, D), BF16),
        grid_spec=pltpu.PrefetchScalarGridSpec(
            num_scalar_prefetch=1,
            grid=(B, n_heads // heads, T // tq),
            in_specs=[pl.BlockSpec((None, tq, w), lambda b, h, i, s: (b, i, h)),
                      pl.BlockSpec((None, T, w), lambda b, h, i, s: (b, 0, h)),
                      pl.BlockSpec((None, T, w), lambda b, h, i, s: (b, 0, h))],
            out_specs=pl.BlockSpec((None, tq, w), lambda b, h, i, s: (b, i, h))),
        compiler_params=_params(("parallel", "parallel", "arbitrary")),
        name="sb_prompt",
    )(bias, q, k, v)


def _sb_decode_kernel(pt_ref, q_ref, bias_ref, ck_hbm, cv_hbm, o_ref,
                      kbuf_ref, vbuf_ref, sel_ref, carry_ref, acc_ref, sem,
                      *, n_heads, n_seq, n_pages, group, scale):
    s = pl.program_id(0)
    g = pl.program_id(1)
    rows, D = kbuf_ref.shape[1:]
    page = rows // group
    dh = D // n_heads
    n_steps = n_pages // group
    step = s * n_steps + g
    slot = step % 2

    def group_copies(seq, grp, sl):
        cps = []
        for i in range(group):
            phys = pt_ref[seq * n_pages + n_pages - (grp + 1) * group + i]
            keys = pl.ds(i * page, page)
            for h in range(n_heads):
                cols = pl.ds(h * dh, dh)
                cps.append(pltpu.make_async_copy(ck_hbm.at[phys, :, h, :], kbuf_ref.at[sl, keys, cols], sem.at[sl]))
                cps.append(pltpu.make_async_copy(cv_hbm.at[phys, :, h, :], vbuf_ref.at[sl, keys, cols], sem.at[sl]))
        return cps

    @pl.when(step == 0)
    def _():
        for cp in group_copies(s, g, slot):
            cp.start()

    @pl.when(step + 1 < n_seq * n_steps)
    def _():
        wrap = g + 1 == n_steps
        for cp in group_copies(jnp.where(wrap, s + 1, s), jnp.where(wrap, 0, g + 1), 1 - slot):
            cp.start()

    @pl.when(g == 0)
    def _():
        hrow = lax.broadcasted_iota(I32, (2 * LANES, D), 0) % LANES
        c = lax.broadcasted_iota(I32, (2 * LANES, D), 1)
        sel_ref[...] = jnp.where((c >= hrow * dh) & (c < (hrow + 1) * dh), 1.0, 0.0).astype(BF16)
        carry_ref[...] = jnp.zeros_like(carry_ref)
        acc_ref[...] = jnp.zeros_like(acc_ref)

    for cp in group_copies(s, g, slot):
        cp.wait()
    kq = kbuf_ref[slot] * q_ref[...]
    lane = lax.broadcasted_iota(I32, (rows, LANES), 1)
    z = jnp.zeros((rows, LANES), F32)
    for h in range(n_heads):
        z = jnp.where(lane == h, jnp.sum(kq[:, h * dh:(h + 1) * dh], axis=1, keepdims=True), z)
    z = z * scale + bias_ref[...]
    lsig, lstay = _log_sigmoid_pair(z, accurate_tail=True)
    row = lax.broadcasted_iota(I32, (rows, rows), 0)
    col = lax.broadcasted_iota(I32, (rows, rows), 1)
    later = jnp.where(col > row, 1.0, 0.0).astype(BF16)
    hi, lo = _split_bf16(lstay)
    la = (jnp.dot(later, hi, preferred_element_type=F32)
          + jnp.dot(later, lo, preferred_element_type=F32) + carry_ref[...])
    a = jnp.exp(lsig + la)
    carry_ref[...] += jnp.sum(lstay, axis=0, keepdims=True)
    a_hi, a_lo = _split_bf16(a)
    ae = jnp.dot(jnp.concatenate([a_hi, a_lo], axis=1), sel_ref[...], preferred_element_type=F32)
    prod = ae * vbuf_ref[slot]
    acc_ref[...] += jnp.sum(prod.reshape(rows // SUBLANES, SUBLANES, D), axis=0)

    @pl.when(g == n_steps - 1)
    def _():
        o_ref[...] = jnp.sum(acc_ref[...], axis=0, keepdims=True)


def _sb_decode(q, cache_k, cache_v, page_table, bias, *, group=2):
    n_seq, D = q.shape
    _, page, n_heads, dh = cache_k.shape
    n_pages = page_table.shape[1]
    group = _pick(n_pages, group)
    rows = group * page
    assert n_heads <= LANES and page % BF16_ROWS == 0
    bias_row = jnp.zeros((1, LANES), F32).at[0, :n_heads].set(bias)

    out = pl.pallas_call(
        functools.partial(_sb_decode_kernel, n_heads=n_heads, n_seq=n_seq, n_pages=n_pages,
                          group=group, scale=dh ** -0.5),
        out_shape=jax.ShapeDtypeStruct((n_seq, 1, D), F32),
        grid_spec=pltpu.PrefetchScalarGridSpec(
            num_scalar_prefetch=1,
            grid=(n_seq, n_pages // group),
            in_specs=[pl.BlockSpec((None, 1, D), lambda s, p, pt: (s, 0, 0)),
                      pl.BlockSpec((1, LANES), lambda s, p, pt: (0, 0)),
                      pl.BlockSpec(memory_space=pl.ANY), pl.BlockSpec(memory_space=pl.ANY)],
            out_specs=pl.BlockSpec((None, 1, D), lambda s, p, pt: (s, 0, 0)),
            scratch_shapes=[pltpu.VMEM((2, rows, D), F32), pltpu.VMEM((2, rows, D), F32),
                            pltpu.VMEM((2 * LANES, D), BF16),
                            pltpu.VMEM((1, LANES), F32), pltpu.VMEM((SUBLANES, D), F32),
                            pltpu.SemaphoreType.DMA((2,))]),
        compiler_params=_params(("arbitrary", "arbitrary")),
        name="sb_decode",
    )(page_table.reshape(-1), q.reshape(n_seq, 1, D), bias_row, cache_k, cache_v)
    return out.reshape(n_seq, D)


def _rmsnorm_kernel(x_ref, g_ref, o_ref):
    o_ref[...] = _rms(x_ref[...], g_ref[...])


def _rmsnorm(x, gamma, *, tm=1024):
    M, D = x.shape
    tm = _pick(M, tm)
    return pl.pallas_call(
        _rmsnorm_kernel,
        out_shape=jax.ShapeDtypeStruct((M, D), F32),
        grid=(M // tm,),
        in_specs=[pl.BlockSpec((tm, D), lambda i: (i, 0)), pl.BlockSpec((1, D), lambda i: (0, 0))],
        out_specs=pl.BlockSpec((tm, D), lambda i: (i, 0)),
        compiler_params=_params(("parallel",)),
        name="rmsnorm",
    )(x, gamma.reshape(1, D))


def kernel(x_prompt, x_sample, state_conv, cache_k, cache_v, page_table, norm_mix, norm_ffn,
           conv_w_in, conv_w, conv_w_out, kv_norm, w_kv, attn_w_q, attn_b, attn_w_o,
           router_group_w, router_group_b, router_expert_w, router_expert_b,
           expert_w_gate, expert_w_up, expert_w_down, final_norm):
    bp, tp, D = x_prompt.shape
    bs, ts, _ = x_sample.shape
    assert ts == 1
    depth = norm_mix.shape[0]
    n_a = conv_w_in.shape[0]
    n_heads = cache_k.shape[2]
    n_groups = router_group_w.shape[2]
    n_exp = router_expert_w.shape[2]
    epg = n_exp // n_groups

    xp = x_prompt.reshape(bp * tp, D)
    xs = x_sample.reshape(bs, D)
    n_r = GROUP_ROWS + n_exp
    wt = jnp.zeros((depth, n_r, D), F32)
    wt = wt.at[:, :n_groups].set(jnp.swapaxes(router_group_w, 1, 2))
    wt = wt.at[:, GROUP_ROWS:].set(jnp.swapaxes(router_expert_w, 1, 2))
    rb = jnp.zeros((depth, n_r, 1), F32)
    rb = rb.at[:, :n_groups, 0].set(router_group_b)
    rb = rb.at[:, GROUP_ROWS:, 0].set(router_expert_b)
    w_kv3 = w_kv[None]

    n_hist = conv_w.shape[1] - 1
    conv_p, conv_s = [], []
    kp = vp = kp_b = vp_b = ks_new = vs_new = None
    for l in range(depth):
        if l < n_a:
            gp, stp = _conv_in(xp, norm_mix[l], conv_w_in, l, conv_w[l], seq_len=tp)
            xp = _linear(gp, conv_w_out, l, res=xp)
            if l == 0 and n_a > 1 and tp >= 2 * n_hist:
                x_win = x_prompt[:, tp - 2 * n_hist:, :]
                bcx_win = _linear(x_win.reshape(bp * 2 * n_hist, D), conv_w_in, l, gamma=norm_mix[l],
                                  hp=True).reshape(bp, 2 * n_hist, 3 * D)
                rows_of = lambda a, i: jnp.concatenate([a[:, j + i] for j in range(n_hist)], axis=0)
                g_tail, _ = _conv_step(rows_of(bcx_win, n_hist), [rows_of(bcx_win, i) for i in range(n_hist)],
                                       conv_w[l], hist_is_bcx=True)
                x_tail = _linear(g_tail, conv_w_out, l, res=rows_of(x_win, n_hist), hp=True)
                tail_rows = jnp.concatenate([jnp.arange(bp, dtype=I32) * tp + tp - n_hist + j
                                             for j in range(n_hist)])
                xp = xp.at[tail_rows].set(x_tail)
            bcx = _linear(xs, conv_w_in, l, gamma=norm_mix[l], hp=True)
            gs, sts = _conv_step(bcx, [state_conv[l][:, i] for i in range(n_hist)], conv_w[l],
                                 hist_is_bcx=False)
            xs = _linear(gs, conv_w_out, l, res=xs, hp=True)
            conv_p.append(stp)
            conv_s.append(sts)
        else:
            j = l - n_a
            qp = _linear(xp, attn_w_q, j, gamma=norm_mix[l], out_dtypes=(BF16,))
            ap = _sb_prompt(qp.reshape(bp, tp, D), kp_b.reshape(bp, tp, D), vp_b.reshape(bp, tp, D),
                            attn_b[j], n_heads=n_heads)
            xp = _linear(ap.reshape(bp * tp, D), attn_w_o, j, res=xp)
            qs = _linear(xs, attn_w_q, j, gamma=norm_mix[l], hp=True)
            a_s = _sb_decode(qs, cache_k, cache_v, page_table, attn_b[j])
            xs = _linear(a_s, attn_w_o, j, res=xs, hp=True)
        moe_args = (l, norm_ffn[l], wt[l], rb[l], expert_w_gate, expert_w_up, expert_w_down)
        xp = _moe(xp, *moe_args, n_groups=n_groups, epg=epg)
        xs = _moe_hp(xs, *moe_args, n_groups=n_groups, epg=epg)
        if l == n_a - 1:
            kp, kp_b = _linear(xp, w_kv3, 0, gamma=kv_norm, col_off=0, n_cols=D, out_dtypes=(F32, BF16))
            vp, vp_b = _linear(xp, w_kv3, 0, gamma=kv_norm, col_off=D, n_cols=D, out_dtypes=(F32, BF16))
            ks_new = _linear(xs, w_kv3, 0, gamma=kv_norm, col_off=0, n_cols=D, hp=True)
            vs_new = _linear(xs, w_kv3, 0, gamma=kv_norm, col_off=D, n_cols=D, hp=True)
    dh = D // n_heads
    y_prompt = _rmsnorm(xp, final_norm).reshape(bp, tp, D)
    y_sample = _rmsnorm(xs, final_norm).reshape(bs, ts, D)
    return (y_prompt, y_sample, jnp.stack(conv_p, axis=0), jnp.stack(conv_s, axis=0),
            kp.reshape(bp, tp, n_heads, dh), vp.reshape(bp, tp, n_heads, dh),
            ks_new.reshape(bs, ts, n_heads, dh), vs_new.reshape(bs, ts, n_heads, dh))
```

```python
import functools

import jax
import jax.numpy as jnp
from jax import lax
from jax.experimental import pallas as pl
from jax.experimental.pallas import tpu as pltpu

F32 = jnp.float32
BF16 = jnp.bfloat16
I32 = jnp.int32

RMS_EPS = 1e-6
TOP_K = 2

LANES = 128
SUBLANES = 8
BF16_ROWS = 16
VMEM_LIMIT = 52 * 1024 * 1024

ROW_TILE = 256
GROUP_ROWS = SUBLANES


def _pick(n, pref):
    t = min(n, pref)
    while n % t:
        t //= 2
    return t


def _params(sem):
    return pltpu.CompilerParams(dimension_semantics=sem, vmem_limit_bytes=VMEM_LIMIT)


def _rms(x, g):
    var = jnp.mean(x * x, axis=-1, keepdims=True)
    return x * lax.rsqrt(var + RMS_EPS) * g


def _split_bf16(x):
    hi = x.astype(BF16)
    lo = (x - hi.astype(F32)).astype(BF16)
    return hi, lo


def _log_sigmoid_pair(z, *, accurate_tail):
    u = jnp.exp(-jnp.abs(z))
    t = jnp.log1p(u) if accurate_tail else jnp.log(1.0 + u)
    return jnp.minimum(z, 0.0) - t, -jnp.maximum(z, 0.0) - t


def _dot_hp(x, w):
    m = x.shape[0]
    assert m % SUBLANES == 0
    x_hi = x.astype(BF16).astype(F32)
    xx = jnp.concatenate([x_hi, x - x_hi], axis=0).astype(BF16)
    w_hi, w_lo = _split_bf16(w)
    r = (jnp.dot(xx, w_hi, preferred_element_type=F32)
         + jnp.dot(xx, w_lo, preferred_element_type=F32))
    return r[:m] + r[m:]


def _linear_kernel(*refs, has_norm, has_res, n_out, hp):
    refs = list(refs)
    x_ref = refs.pop(0)
    g_ref = refs.pop(0) if has_norm else None
    w_ref = refs.pop(0)
    r_ref = refs.pop(0) if has_res else None
    out_refs = refs[:n_out]
    if has_norm:
        h_ref = refs[n_out]

        @pl.when(pl.program_id(1) == 0)
        def _():
            h_ref[...] = _rms(x_ref[...], g_ref[...]).astype(h_ref.dtype)

        h = h_ref[...]
    else:
        h = x_ref[...]
    if hp:
        acc = _dot_hp(h.astype(F32), w_ref[...])
    else:
        acc = jnp.dot(h.astype(BF16), w_ref[...].astype(BF16), preferred_element_type=F32)
    if has_res:
        acc = acc + r_ref[...]
    for o in out_refs:
        o[...] = acc.astype(o.dtype)


def _linear(x, w3, l, *, gamma=None, res=None, col_off=0, n_cols=None,
            out_dtypes=(F32,), tm=1024, tn=512, hp=False):
    m_rows, K = x.shape
    if hp and m_rows % SUBLANES:
        pad = ((0, SUBLANES - m_rows % SUBLANES), (0, 0))
        x = jnp.pad(x, pad)
        res = None if res is None else jnp.pad(res, pad)
    M = x.shape[0]
    n_cols = n_cols or w3.shape[2]
    tm = _pick(M, tm)
    tn = _pick(n_cols, tn)
    assert col_off % tn == 0
    jo = col_off // tn
    in_specs = [pl.BlockSpec((tm, K), lambda i, j: (i, 0))]
    args = [x]
    if gamma is not None:
        in_specs.append(pl.BlockSpec((1, K), lambda i, j: (0, 0)))
        args.append(gamma.reshape(1, K))
    in_specs.append(pl.BlockSpec((None, K, tn), lambda i, j: (l, 0, j + jo)))
    args.append(w3)
    if res is not None:
        in_specs.append(pl.BlockSpec((tm, tn), lambda i, j: (i, j)))
        args.append(res)
    scratch = []
    if gamma is not None:
        scratch.append(pltpu.VMEM((tm, K), F32 if hp or tm % BF16_ROWS else BF16))
    outs = pl.pallas_call(
        functools.partial(_linear_kernel, has_norm=gamma is not None,
                          has_res=res is not None, n_out=len(out_dtypes), hp=hp),
        out_shape=[jax.ShapeDtypeStruct((M, n_cols), dt) for dt in out_dtypes],
        grid=(M // tm, n_cols // tn),
        in_specs=in_specs,
        out_specs=[pl.BlockSpec((tm, tn), lambda i, j: (i, j)) for _ in out_dtypes],
        scratch_shapes=scratch,
        compiler_params=_params(("parallel", "arbitrary")),
        name="linear",
    )(*args)
    outs = [o[:m_rows] for o in outs] if M != m_rows else outs
    return outs[0] if len(out_dtypes) == 1 else outs


def _conv_in_kernel(x_ref, g_ref, wb_ref, wc_ref, wx_ref, cw_ref, gout_ref, st_ref,
                    h_ref, carry_ref, *, tiles_per_seq, tn, conv_w):
    i = pl.program_id(0)
    j = pl.program_id(1)

    @pl.when(j == 0)
    def _():
        h_ref[...] = _rms(x_ref[...], g_ref[...]).astype(BF16)

    h = h_ref[...]
    b = jnp.dot(h, wb_ref[...].astype(BF16), preferred_element_type=F32)
    c = jnp.dot(h, wc_ref[...].astype(BF16), preferred_element_type=F32)
    xin = jnp.dot(h, wx_ref[...].astype(BF16), preferred_element_type=F32)
    u = c * xin
    tm = u.shape[0]
    cols = pl.ds(pl.multiple_of(j * tn, tn), tn)
    first = (i % tiles_per_seq) == 0
    prev = jnp.where(first, 0.0, carry_ref[:, cols])
    cw = cw_ref[...]
    row8 = lax.broadcasted_iota(I32, (SUBLANES, tn), 0)
    conv_top = cw[conv_w - 1:conv_w, :] * u[:SUBLANES]
    conv_rest = cw[conv_w - 1:conv_w, :] * u[SUBLANES:]
    for s in range(1, conv_w):
        wrow = cw[conv_w - 1 - s:conv_w - s, :]
        shifted = pltpu.roll(u, s, 0)
        top = jnp.where(row8 < s, pltpu.roll(prev, s, 0), shifted[:SUBLANES])
        conv_top = conv_top + wrow * top
        conv_rest = conv_rest + wrow * shifted[SUBLANES:]
    conv = jnp.concatenate([conv_top, conv_rest], axis=0)
    gout_ref[...] = (b * conv).astype(gout_ref.dtype)
    last8 = u[tm - SUBLANES:, :]
    carry_ref[:, cols] = last8
    st_ref[...] = last8


def _conv_in(x, gamma, w_in3, l, cw, *, seq_len, tm=1024, tn=256):
    M, D = x.shape
    conv_w = cw.shape[0]
    tm = _pick(seq_len, tm)
    tn = _pick(D, tn)
    nD = D // tn
    assert conv_w - 1 <= SUBLANES and tm >= 2 * SUBLANES
    tiles_per_seq = seq_len // tm
    g, st = pl.pallas_call(
        functools.partial(_conv_in_kernel, tiles_per_seq=tiles_per_seq, tn=tn, conv_w=conv_w),
        out_shape=[jax.ShapeDtypeStruct((M, D), BF16),
                   jax.ShapeDtypeStruct((M // tm, SUBLANES, D), F32)],
        grid=(M // tm, nD),
        in_specs=[pl.BlockSpec((tm, D), lambda i, j: (i, 0)),
                  pl.BlockSpec((1, D), lambda i, j: (0, 0)),
                  pl.BlockSpec((None, D, tn), lambda i, j: (l, 0, j)),
                  pl.BlockSpec((None, D, tn), lambda i, j: (l, 0, j + nD)),
                  pl.BlockSpec((None, D, tn), lambda i, j: (l, 0, j + 2 * nD)),
                  pl.BlockSpec((conv_w, tn), lambda i, j: (0, j))],
        out_specs=[pl.BlockSpec((tm, tn), lambda i, j: (i, j)),
                   pl.BlockSpec((None, SUBLANES, tn), lambda i, j: (i, 0, j))],
        scratch_shapes=[pltpu.VMEM((tm, D), BF16), pltpu.VMEM((SUBLANES, D), F32)],
        compiler_params=_params(("arbitrary", "arbitrary")),
        name="conv_in",
    )(x, gamma.reshape(1, D), w_in3, w_in3, w_in3, cw)
    return g, st[tiles_per_seq - 1::tiles_per_seq, SUBLANES - (conv_w - 1):, :]


def _conv_step_kernel(bcx_ref, *refs, d, conv_w, hist_is_bcx):
    hist_refs = refs[:conv_w - 1]
    cw_ref, g_ref, nst_ref = refs[conv_w - 1:]

    def conv_input(ref):
        return ref[:, d:2 * d] * ref[:, 2 * d:3 * d]

    b = bcx_ref[:, 0:d]
    u_ext = [conv_input(r) if hist_is_bcx else r[...] for r in hist_refs] + [conv_input(bcx_ref)]
    conv = cw_ref[0:1, :] * u_ext[0]
    for i in range(1, conv_w):
        conv = conv + cw_ref[i:i + 1, :] * u_ext[i]
    g_ref[...] = b * conv
    for i in range(conv_w - 1):
        nst_ref[:, i * d:(i + 1) * d] = u_ext[i + 1]


def _conv_step(bcx, hist, cw, *, hist_is_bcx):
    n = bcx.shape[0]
    conv_w, d = cw.shape
    g, nst = pl.pallas_call(
        functools.partial(_conv_step_kernel, d=d, conv_w=conv_w, hist_is_bcx=hist_is_bcx),
        out_shape=[jax.ShapeDtypeStruct((n, d), F32),
                   jax.ShapeDtypeStruct((n, (conv_w - 1) * d), F32)],
        name="conv_step",
    )(bcx, *hist, cw)
    return g, nst.reshape(n, conv_w - 1, d)


def _router_kernel(x_ref, g_ref, wt_ref, bias_ref, base_ref, ut_ref,
                   h_ref, ids_ref, rank_ref, wts_ref, cnt_ref,
                   *, n_groups, epg, n_valid, tm):
    i = pl.program_id(0)

    @pl.when(i == 0)
    def _():
        cnt_ref[...] = base_ref[...]

    hn = _rms(x_ref[...], g_ref[...])
    h_ref[...] = hn
    h_hi, h_lo = _split_bf16(hn)
    w_hi, w_lo = _split_bf16(wt_ref[...])
    nt = (((1,), (1,)), ((), ()))
    logits = (lax.dot_general(w_hi, h_hi, nt, preferred_element_type=F32)
              + lax.dot_general(w_hi, h_lo, nt, preferred_element_type=F32)
              + lax.dot_general(w_lo, h_hi, nt, preferred_element_type=F32))
    logits = logits + bias_ref[...]
    n_rows = logits.shape[0]
    rows = lax.broadcasted_iota(I32, (n_rows, tm), 0)
    big = jnp.int32(n_rows)
    neg = jnp.float32(-jnp.inf)

    is_g = rows < n_groups
    lg = jnp.where(is_g, logits, neg)
    eg = jnp.exp(lg - jnp.max(lg, axis=0, keepdims=True))
    pg = eg / jnp.sum(eg, axis=0, keepdims=True)
    g_val = jnp.max(pg, axis=0, keepdims=True)
    g_idx = jnp.min(jnp.where((pg == g_val) & is_g, rows, big), axis=0, keepdims=True)

    lo_row = GROUP_ROWS + g_idx * epg
    in_grp = (rows >= lo_row) & (rows < lo_row + epg)
    le = jnp.where(in_grp, logits, neg)
    ee = jnp.exp(le - jnp.max(le, axis=0, keepdims=True))
    pe = jnp.where(in_grp, ee / jnp.sum(ee, axis=0, keepdims=True), -1.0)
    v1 = jnp.max(pe, axis=0, keepdims=True)
    i1 = jnp.min(jnp.where(pe == v1, rows, big), axis=0, keepdims=True)
    pe2 = jnp.where(rows == i1, -1.0, pe)
    v2 = jnp.max(pe2, axis=0, keepdims=True)
    i2 = jnp.min(jnp.where(pe2 == v2, rows, big), axis=0, keepdims=True)
    den = v1 + v2
    wts_ref[0:1, :] = g_val * (v1 / den)
    wts_ref[1:2, :] = g_val * (v2 / den)
    ids_ref[0:1, :] = i1 - GROUP_ROWS
    ids_ref[1:2, :] = i2 - GROUP_ROWS

    tok = i * tm + lax.broadcasted_iota(I32, (n_rows, tm), 1)
    sel1 = rows == i1
    sel2 = rows == i2
    oh = jnp.where((sel1 | sel2) & (tok < n_valid), 1.0, 0.0)
    cum = jnp.dot(oh.astype(BF16), ut_ref[...], preferred_element_type=F32) + cnt_ref[...]
    rank_ref[0:1, :] = (jnp.sum(jnp.where(sel1, cum, 0.0), axis=0, keepdims=True) - 1.0).astype(I32)
    rank_ref[1:2, :] = (jnp.sum(jnp.where(sel2, cum, 0.0), axis=0, keepdims=True) - 1.0).astype(I32)
    cnt_ref[...] += jnp.sum(oh, axis=1, keepdims=True)


def _router(x, gamma, wt, bias, base, *, n_groups, epg, n_valid, tm=512):
    M, D = x.shape
    tm = _pick(M, tm)
    R = wt.shape[0]
    ut = jnp.triu(jnp.ones((tm, tm), BF16))
    full = lambda shape: pl.BlockSpec(shape, lambda i: (0,) * len(shape))
    tokspec = lambda dt: (jax.ShapeDtypeStruct((TOP_K, M), dt), pl.BlockSpec((TOP_K, tm), lambda i: (0, i)))
    (ids_s, ids_b), (rank_s, rank_b), (wts_s, wts_b) = tokspec(I32), tokspec(I32), tokspec(F32)
    return pl.pallas_call(
        functools.partial(_router_kernel, n_groups=n_groups, epg=epg, n_valid=n_valid, tm=tm),
        out_shape=[jax.ShapeDtypeStruct((M, D), F32), ids_s, rank_s, wts_s,
                   jax.ShapeDtypeStruct((R, 1), F32)],
        grid=(M // tm,),
        in_specs=[pl.BlockSpec((tm, D), lambda i: (i, 0)), full((1, D)), full((R, D)),
                  full((R, 1)), full((R, 1)), full((tm, tm))],
        out_specs=[pl.BlockSpec((tm, D), lambda i: (i, 0)), ids_b, rank_b, wts_b, full((R, 1))],
        compiler_params=_params(("arbitrary",)),
        name="router",
    )(x, gamma.reshape(1, D), wt, bias, base, ut)


def _dispatch_kernel(pos_ref, h_ref, xs_in_hbm, xs_hbm, sem, *, tc, n_tok):
    del xs_in_hbm
    base = pl.program_id(0) * tc

    def row_copy(t, p):
        return pltpu.make_async_copy(h_ref.at[pl.ds(t, 1), :], xs_hbm.at[pl.ds(p, 1), :], sem)

    def issue(t, c):
        for k in range(TOP_K):
            row_copy(t, pos_ref[k * n_tok + base + t]).start()
        return c

    lax.fori_loop(0, tc, issue, 0, unroll=8)

    def drain(t, c):
        row_copy(0, 0).wait()
        return c

    lax.fori_loop(0, TOP_K * tc, drain, 0, unroll=8)


def _dispatch(h, pos, xs, *, tc=512):
    n_tok, D = h.shape
    tc = _pick(n_tok, tc)
    return pl.pallas_call(
        functools.partial(_dispatch_kernel, tc=tc, n_tok=n_tok),
        out_shape=jax.ShapeDtypeStruct(xs.shape, xs.dtype),
        grid_spec=pltpu.PrefetchScalarGridSpec(
            num_scalar_prefetch=1,
            grid=(n_tok // tc,),
            in_specs=[pl.BlockSpec((tc, D), lambda i, p: (i, 0)), pl.BlockSpec(memory_space=pl.ANY)],
            out_specs=pl.BlockSpec(memory_space=pl.ANY),
            scratch_shapes=[pltpu.SemaphoreType.DMA(())]),
        input_output_aliases={2: 0},
        compiler_params=_params(("arbitrary",)),
        name="dispatch",
    )(pos.reshape(-1), h, xs)


def _gate_up_kernel(te_ref, nu_ref, xs_ref, wg_ref, wu_ref, act_ref, wgb_ref, wub_ref):
    r = pl.program_id(1)
    used = r < nu_ref[0]
    new_expert = (r == 0) | (te_ref[r] != te_ref[jnp.maximum(r - 1, 0)])

    @pl.when(used & new_expert)
    def _():
        wgb_ref[...] = wg_ref[...].astype(BF16)
        wub_ref[...] = wu_ref[...].astype(BF16)

    @pl.when(used)
    def _():
        x = xs_ref[...].astype(BF16)
        g = jnp.dot(x, wgb_ref[...], preferred_element_type=F32)
        u = jnp.dot(x, wub_ref[...], preferred_element_type=F32)
        act_ref[...] = (g / (1.0 + jnp.exp(-g)) * u).astype(act_ref.dtype)

    @pl.when(jnp.logical_not(used))
    def _():
        act_ref[...] = jnp.zeros_like(act_ref)


def _down_kernel(te_ref, nu_ref, act_ref, wd_ref, out_ref, wdb_ref):
    r = pl.program_id(1)
    used = r < nu_ref[0]
    new_expert = (r == 0) | (te_ref[r] != te_ref[jnp.maximum(r - 1, 0)])

    @pl.when(used & new_expert)
    def _():
        wdb_ref[...] = wd_ref[...].astype(BF16)

    @pl.when(used)
    def _():
        out_ref[...] = jnp.dot(act_ref[...], wdb_ref[...], preferred_element_type=F32)

    @pl.when(jnp.logical_not(used))
    def _():
        out_ref[...] = jnp.zeros_like(out_ref)


def _experts(xs, w_gate, w_up, w_down, l, tile_expert, n_used, *, tr, tf=1024, tn=2048):
    R, D = xs.shape
    F = w_gate.shape[3]
    tf = _pick(F, tf)
    tn = _pick(D, tn)
    nt = R // tr
    row = lambda f, r, te, nu: jnp.minimum(r, nu[0] - 1)
    act = pl.pallas_call(
        _gate_up_kernel,
        out_shape=jax.ShapeDtypeStruct((R, F), BF16),
        grid_spec=pltpu.PrefetchScalarGridSpec(
            num_scalar_prefetch=2,
            grid=(F // tf, nt),
            in_specs=[pl.BlockSpec((tr, D), lambda f, r, te, nu: (row(f, r, te, nu), 0)),
                      pl.BlockSpec((None, None, D, tf), lambda f, r, te, nu: (l, te[r], 0, f)),
                      pl.BlockSpec((None, None, D, tf), lambda f, r, te, nu: (l, te[r], 0, f))],
            out_specs=pl.BlockSpec((tr, tf), lambda f, r, te, nu: (r, f)),
            scratch_shapes=[pltpu.VMEM((D, tf), BF16), pltpu.VMEM((D, tf), BF16)]),
        compiler_params=_params(("arbitrary", "arbitrary")),
        name="expert_gate_up",
    )(tile_expert, n_used, xs, w_gate, w_up)
    return pl.pallas_call(
        _down_kernel,
        out_shape=jax.ShapeDtypeStruct((R, D), F32),
        grid_spec=pltpu.PrefetchScalarGridSpec(
            num_scalar_prefetch=2,
            grid=(D // tn, nt),
            in_specs=[pl.BlockSpec((tr, F), lambda n, r, te, nu: (row(n, r, te, nu), 0)),
                      pl.BlockSpec((None, None, F, tn), lambda n, r, te, nu: (l, te[r], 0, n))],
            out_specs=pl.BlockSpec((tr, tn), lambda n, r, te, nu: (r, n)),
            scratch_shapes=[pltpu.VMEM((F, tn), BF16)]),
        compiler_params=_params(("arbitrary", "arbitrary")),
        name="expert_down",
    )(tile_expert, n_used, act, w_down)


def _combine_kernel(pos_ref, x_ref, w_ref, eo_hbm, out_ref, buf_ref, sem, *, tm, m_pad, n_tiles):
    i = pl.program_id(0)
    slot = i % 2

    def row_copy(sl, k, t, p):
        return pltpu.make_async_copy(eo_hbm.at[pl.ds(p, 1), :], buf_ref.at[sl, k, pl.ds(t, 1), :], sem.at[sl])

    def gather_tile(tile, sl):
        def issue(t, c):
            for k in range(TOP_K):
                row_copy(sl, k, t, pos_ref[k * m_pad + tile * tm + t]).start()
            return c

        lax.fori_loop(0, tm, issue, 0, unroll=8)

    @pl.when(i == 0)
    def _():
        gather_tile(i, slot)

    @pl.when(i + 1 < n_tiles)
    def _():
        gather_tile(i + 1, 1 - slot)

    def drain(t, c):
        row_copy(slot, 0, 0, 0).wait()
        return c

    lax.fori_loop(0, TOP_K * tm, drain, 0, unroll=8)
    w = w_ref[...]
    acc = x_ref[...]
    for k in range(TOP_K):
        acc = acc + w[:, k:k + 1] * buf_ref[slot, k]
    out_ref[...] = acc


def _combine(x, eo, pos, wts, *, tm=256):
    M, D = x.shape
    m_pad = pos.shape[1]
    tm = _pick(M, tm)
    return pl.pallas_call(
        functools.partial(_combine_kernel, tm=tm, m_pad=m_pad, n_tiles=M // tm),
        out_shape=jax.ShapeDtypeStruct((M, D), F32),
        grid_spec=pltpu.PrefetchScalarGridSpec(
            num_scalar_prefetch=1,
            grid=(M // tm,),
            in_specs=[pl.BlockSpec((tm, D), lambda i, p: (i, 0)),
                      pl.BlockSpec((tm, TOP_K), lambda i, p: (i, 0)),
                      pl.BlockSpec(memory_space=pl.ANY)],
            out_specs=pl.BlockSpec((tm, D), lambda i, p: (i, 0)),
            scratch_shapes=[pltpu.VMEM((2, TOP_K, tm, D), F32), pltpu.SemaphoreType.DMA((2,))]),
        compiler_params=_params(("arbitrary",)),
        name="combine",
    )(pos.reshape(-1), x, wts, eo)


def _moe(x, l, norm, wt, bias, w_gate, w_up, w_down, *, n_groups, epg):
    n_tok, D = x.shape
    n_exp = n_groups * epg
    h, ids, rank, wts, cnt = _router(x, norm, wt, bias, jnp.zeros((wt.shape[0], 1), F32),
                                     n_groups=n_groups, epg=epg, n_valid=n_tok)
    tr = ROW_TILE
    counts = cnt[GROUP_ROWS:GROUP_ROWS + n_exp, 0].astype(I32)
    tiles = (counts + tr - 1) // tr
    tile_end = jnp.cumsum(tiles)
    row_off = (tile_end - tiles) * tr
    n_used = tile_end[-1:]
    nt_max = -(-TOP_K * n_tok // tr) + n_exp
    tile_id = jnp.minimum(jnp.arange(nt_max, dtype=I32), n_used - 1)
    tile_expert = jnp.sum((tile_end[None, :] <= tile_id[:, None]).astype(I32), axis=1)
    expert_iota = jnp.arange(n_exp, dtype=I32)[:, None, None]
    pos = jnp.sum(jnp.where(ids[None] == expert_iota, row_off[:, None, None], 0), axis=0) + rank
    xsort = _dispatch(h, pos, jnp.zeros((nt_max * tr, D), F32))
    eo = _experts(xsort, w_gate, w_up, w_down, l, tile_expert, n_used, tr=tr)
    return _combine(x, eo, pos, wts.T)


def _pair_expert_kernel(pe_ref, pt_ref, pw_ref, h_ref, x_ref, wg_ref, wu_ref, wd_ref, out_ref):
    del pe_ref
    f = pl.program_id(0)
    p = pl.program_id(1)

    @pl.when((f == 0) & (p == 0))
    def _():
        out_ref[...] = x_ref[...]

    h = h_ref[...]
    g = _dot_hp(h, wg_ref[...])
    u = _dot_hp(h, wu_ref[...])
    y = _dot_hp(g / (1.0 + jnp.exp(-g)) * u, wd_ref[...])
    rows = lax.broadcasted_iota(I32, (h.shape[0], 1), 0)
    out_ref[...] += jnp.where(rows == pt_ref[p], pw_ref[p], 0.0) * y


def _moe_hp(x, l, norm, wt, bias, w_gate, w_up, w_down, *, n_groups, epg, tf=256):
    n_tok, D = x.shape
    F = w_gate.shape[3]
    tf = _pick(F, tf)
    m_pad = -(-n_tok // LANES) * LANES
    h, ids, _, wts, _ = _router(jnp.pad(x, ((0, m_pad - n_tok), (0, 0))), norm, wt, bias,
                                jnp.zeros((wt.shape[0], 1), F32),
                                n_groups=n_groups, epg=epg, n_valid=n_tok)
    n_pairs = TOP_K * n_tok
    pair_e = ids[:, :n_tok].reshape(-1)
    pair_t = jnp.tile(jnp.arange(n_tok, dtype=I32), TOP_K)
    pair_w = wts[:, :n_tok].reshape(-1)
    pair_iota = jnp.arange(n_pairs, dtype=I32)
    key = pair_e * n_pairs + pair_iota
    slot = jnp.sum((key[None, :] < key[:, None]).astype(I32), axis=1)
    place = lambda v: jnp.sum(jnp.where(slot[None, :] == pair_iota[:, None], v[None, :], 0), axis=1)
    pair_e, pair_t, pair_w = place(pair_e), place(pair_t), place(pair_w)
    full = pl.BlockSpec((n_tok, D), lambda f, p, pe, pt, pw: (0, 0))
    return pl.pallas_call(
        _pair_expert_kernel,
        out_shape=jax.ShapeDtypeStruct((n_tok, D), F32),
        grid_spec=pltpu.PrefetchScalarGridSpec(
            num_scalar_prefetch=3,
            grid=(F // tf, n_pairs),
            in_specs=[full, full,
                      pl.BlockSpec((None, None, D, tf), lambda f, p, pe, pt, pw: (l, pe[p], 0, f)),
                      pl.BlockSpec((None, None, D, tf), lambda f, p, pe, pt, pw: (l, pe[p], 0, f)),
                      pl.BlockSpec((None, None, tf, D), lambda f, p, pe, pt, pw: (l, pe[p], f, 0))],
            out_specs=full),
        compiler_params=_params(("arbitrary", "arbitrary")),
        name="pair_experts",
    )(pair_e, pair_t, pair_w, h[:n_tok], x, w_gate, w_up, w_down)


def _sb_prompt_kernel(bias_ref, q_ref, k_ref, v_ref, o_ref, *, tq, dh, heads, scale):
    hg = pl.program_id(1)
    qi = pl.program_id(2)
    row = lax.broadcasted_iota(I32, (tq, tq), 0)
    col = lax.broadcasted_iota(I32, (tq, tq), 1)
    later = jnp.where(row > col, 1.0, 0.0).astype(BF16)
    later2 = jnp.concatenate([later, later], axis=0)
    diag = col < row
    nt = (((1,), (1,)), ((), ()))
    lanes =[slice(i * dh, (i + 1) * dh) for i in range(heads)]
    qs = [q_ref[:, ln] for ln in lanes]
    biases = [bias_ref[hg * heads + i] for i in range(heads)]

    def tile(i, j, c, acc, mask):
        ks = pl.ds(pl.multiple_of(j * tq, tq), tq)
        z = lax.dot_general(qs[i], k_ref[ks, lanes[i]], nt, preferred_element_type=F32) * scale + biases[i]
        lsig, lstay = _log_sigmoid_pair(z, accurate_tail=False)
        if mask is not None:
            lstay = jnp.where(mask, lstay, 0.0)
        la = jnp.dot(lstay.astype(BF16), later, preferred_element_type=F32) + c
        a = jnp.exp(lsig + la)
        if mask is not None:
            a = jnp.where(mask, a, 0.0)
        acc = acc + jnp.dot(a.astype(BF16), v_ref[ks, lanes[i]], preferred_element_type=F32)
        return c + jnp.sum(lstay, axis=1, keepdims=True), acc

    init = tuple(tile(i, qi, jnp.zeros((tq, 1), F32), jnp.zeros((tq, dh), F32), diag)
                 for i in range(heads))

    def step(jj, carry):
        return tuple(tile(i, qi - jj, *carry[i], None) for i in range(heads))

    out = lax.fori_loop(1, qi + 1, step, init)
    for i in range(heads):
        o_ref[:, lanes[i]] = out[i][1].astype(o_ref.dtype)


def _sb_prompt(q, k, v, bias, *, n_heads, tq=512):
    B, T, D = q.shape
    dh = D // n_heads
    tq = _pick(T, tq)
    heads = 2 if n_heads % 2 == 0 else 1
    w = heads * dh
    return pl.pallas_call(
        functools.partial(_sb_prompt_kernel, tq=tq, dh=dh, heads=heads, scale=dh ** -0.5),
        out_shape=jax.ShapeDtypeStruct((B, T, D), BF16),
        grid_spec=pltpu.PrefetchScalarGridSpec(
            num_scalar_prefetch=1,
            grid=(B, n_heads // heads, T // tq),
            in_specs=[pl.BlockSpec((None, tq, w), lambda b, h, i, s: (b, i, h)),
                      pl.BlockSpec((None, T, w), lambda b, h, i, s: (b, 0, h)),
                      pl.BlockSpec((None, T, w), lambda b, h, i, s: (b, 0, h))],
            out_specs=pl.BlockSpec((None, tq, w), lambda b, h, i, s: (b, i, h))),
        compiler_params=_params(("parallel", "parallel", "arbitrary")),
        name="sb_prompt",
    )(bias, q, k, v)


def _sb_decode_kernel(pt_ref, q_ref, bias_ref, ck_hbm, cv_hbm, o_ref,
                      kbuf_ref, vbuf_ref, sel_ref, carry_ref, acc_ref, sem,
                      *, n_heads, n_seq, n_pages, group, scale):
    s = pl.program_id(0)
    g = pl.program_id(1)
    rows, D = kbuf_ref.shape[1:]
    page = rows // group
    dh = D // n_heads
    n_steps = n_pages // group
    step = s * n_steps + g
    slot = step % 2

    def group_copies(seq, grp, sl):
        cps = []
        for i in range(group):
            phys = pt_ref[seq * n_pages + n_pages - (grp + 1) * group + i]
            keys = pl.ds(i * page, page)
            for h in range(n_heads):
                cols = pl.ds(h * dh, dh)
                cps.append(pltpu.make_async_copy(ck_hbm.at[phys, :, h, :], kbuf_ref.at[sl, keys, cols], sem.at[sl]))
                cps.append(pltpu.make_async_copy(cv_hbm.at[phys, :, h, :], vbuf_ref.at[sl, keys, cols], sem.at[sl]))
        return cps

    @pl.when(step == 0)
    def _():
        for cp in group_copies(s, g, slot):
            cp.start()

    @pl.when(step + 1 < n_seq * n_steps)
    def _():
        wrap = g + 1 == n_steps
        for cp in group_copies(jnp.where(wrap, s + 1, s), jnp.where(wrap, 0, g + 1), 1 - slot):
            cp.start()

    @pl.when(g == 0)
    def _():
        hrow = lax.broadcasted_iota(I32, (2 * LANES, D), 0) % LANES
        c = lax.broadcasted_iota(I32, (2 * LANES, D), 1)
        sel_ref[...] = jnp.where((c >= hrow * dh) & (c < (hrow + 1) * dh), 1.0, 0.0).astype(BF16)
        carry_ref[...] = jnp.zeros_like(carry_ref)
        acc_ref[...] = jnp.zeros_like(acc_ref)

    for cp in group_copies(s, g, slot):
        cp.wait()
    lane = lax.broadcasted_iota(I32, (page, LANES), 1)
    row = lax.broadcasted_iota(I32, (page, page), 0)
    col = lax.broadcasted_iota(I32, (page, page), 1)
    later = jnp.where(col > row, 1.0, 0.0).astype(BF16)
    carry = carry_ref[...]
    acc = acc_ref[...]
    for i in reversed(range(group)):
        keys = pl.ds(i * page, page)
        kq = kbuf_ref[slot, keys, :] * q_ref[...]
        z = jnp.zeros((page, LANES), F32)
        for h in range(n_heads):
            z = jnp.where(lane == h, jnp.sum(kq[:, h * dh:(h + 1) * dh], axis=1, keepdims=True), z)
        z = z * scale + bias_ref[...]
        lsig, lstay = _log_sigmoid_pair(z, accurate_tail=True)
        hi, lo = _split_bf16(lstay)
        la = (jnp.dot(later, hi, preferred_element_type=F32)
              + jnp.dot(later, lo, preferred_element_type=F32) + carry)
        a = jnp.exp(lsig + la)
        carry = carry + jnp.sum(lstay, axis=0, keepdims=True)
        a_hi, a_lo = _split_bf16(a)
        ae = jnp.dot(jnp.concatenate([a_hi, a_lo], axis=1), sel_ref[...], preferred_element_type=F32)
        prod = ae * vbuf_ref[slot, keys, :]
        acc = acc + jnp.sum(prod.reshape(page // SUBLANES, SUBLANES, D), axis=0)
    carry_ref[...] = carry
    acc_ref[...] = acc

    @pl.when(g == n_steps - 1)
    def _():
        o_ref[...] = jnp.sum(acc_ref[...], axis=0, keepdims=True)


def _sb_decode(q, cache_k, cache_v, page_table, bias, *, group=4):
    n_seq, D = q.shape
    _, page, n_heads, dh = cache_k.shape
    n_pages = page_table.shape[1]
    group = _pick(n_pages, group)
    rows = group * page
    assert n_heads <= LANES and page % BF16_ROWS == 0
    bias_row = jnp.zeros((1, LANES), F32).at[0, :n_heads].set(bias)

    out = pl.pallas_call(
        functools.partial(_sb_decode_kernel, n_heads=n_heads, n_seq=n_seq, n_pages=n_pages,
                          group=group, scale=dh ** -0.5),
        out_shape=jax.ShapeDtypeStruct((n_seq, 1, D), F32),
        grid_spec=pltpu.PrefetchScalarGridSpec(
            num_scalar_prefetch=1,
            grid=(n_seq, n_pages // group),
            in_specs=[pl.BlockSpec((None, 1, D), lambda s, p, pt: (s, 0, 0)),
                      pl.BlockSpec((1, LANES), lambda s, p, pt: (0, 0)),
                      pl.BlockSpec(memory_space=pl.ANY), pl.BlockSpec(memory_space=pl.ANY)],
            out_specs=pl.BlockSpec((None, 1, D), lambda s, p, pt: (s, 0, 0)),
            scratch_shapes=[pltpu.VMEM((2, rows, D), F32), pltpu.VMEM((2, rows, D), F32),
                            pltpu.VMEM((2 * LANES, D), BF16),
                            pltpu.VMEM((1, LANES), F32), pltpu.VMEM((SUBLANES, D), F32),
                            pltpu.SemaphoreType.DMA((2,))]),
        compiler_params=_params(("arbitrary", "arbitrary")),
        name="sb_decode",
    )(page_table.reshape(-1), q.reshape(n_seq, 1, D), bias_row, cache_k, cache_v)
    return out.reshape(n_seq, D)


def _rmsnorm_kernel(x_ref, g_ref, o_ref):
    o_ref[...] = _rms(x_ref[...], g_ref[...])


def _rmsnorm(x, gamma, *, tm=1024):
    M, D = x.shape
    tm = _pick(M, tm)
    return pl.pallas_call(
        _rmsnorm_kernel,
        out_shape=jax.ShapeDtypeStruct((M, D), F32),
        grid=(M // tm,),
        in_specs=[pl.BlockSpec((tm, D), lambda i: (i, 0)), pl.BlockSpec((1, D), lambda i: (0, 0))],
        out_specs=pl.BlockSpec((tm, D), lambda i: (i, 0)),
        compiler_params=_params(("parallel",)),
        name="rmsnorm",
    )(x, gamma.reshape(1, D))


def kernel(x_prompt, x_sample, state_conv, cache_k, cache_v, page_table, norm_mix, norm_ffn,
           conv_w_in, conv_w, conv_w_out, kv_norm, w_kv, attn_w_q, attn_b, attn_w_o,
           router_group_w, router_group_b, router_expert_w, router_expert_b,
           expert_w_gate, expert_w_up, expert_w_down, final_norm):
    bp, tp, D = x_prompt.shape
    bs, ts, _ = x_sample.shape
    assert ts == 1
    depth = norm_mix.shape[0]
    n_a = conv_w_in.shape[0]
    n_heads = cache_k.shape[2]
    n_groups = router_group_w.shape[2]
    n_exp = router_expert_w.shape[2]
    epg = n_exp // n_groups

    xp = x_prompt.reshape(bp * tp, D)
    xs = x_sample.reshape(bs, D)
    n_r = GROUP_ROWS + n_exp
    wt = jnp.zeros((depth, n_r, D), F32)
    wt = wt.at[:, :n_groups].set(jnp.swapaxes(router_group_w, 1, 2))
    wt = wt.at[:, GROUP_ROWS:].set(jnp.swapaxes(router_expert_w, 1, 2))
    rb = jnp.zeros((depth, n_r, 1), F32)
    rb = rb.at[:, :n_groups, 0].set(router_group_b)
    rb = rb.at[:, GROUP_ROWS:, 0].set(router_expert_b)
    w_kv3 = w_kv[None]

    n_hist = conv_w.shape[1] - 1
    conv_p, conv_s = [], []
    kp = vp = kp_b = vp_b = ks_new = vs_new = None
    for l in range(depth):
        if l < n_a:
            gp, stp = _conv_in(xp, norm_mix[l], conv_w_in, l, conv_w[l], seq_len=tp)
            xp = _linear(gp, conv_w_out, l, res=xp)
            if l == 0 and n_a > 1 and tp >= 2 * n_hist:
                x_win = x_prompt[:, tp - 2 * n_hist:, :]
                bcx_win = _linear(x_win.reshape(bp * 2 * n_hist, D), conv_w_in, l, gamma=norm_mix[l],
                                  hp=True).reshape(bp, 2 * n_hist, 3 * D)
                rows_of = lambda a, i: jnp.concatenate([a[:, j + i] for j in range(n_hist)], axis=0)
                g_tail, _ = _conv_step(rows_of(bcx_win, n_hist), [rows_of(bcx_win, i) for i in range(n_hist)],
                                       conv_w[l], hist_is_bcx=True)
                x_tail = _linear(g_tail, conv_w_out, l, res=rows_of(x_win, n_hist), hp=True)
                tail_rows = jnp.concatenate([jnp.arange(bp, dtype=I32) * tp + tp - n_hist + j
                                             for j in range(n_hist)])
                xp = xp.at[tail_rows].set(x_tail)
            bcx = _linear(xs, conv_w_in, l, gamma=norm_mix[l], hp=True)
            gs, sts = _conv_step(bcx, [state_conv[l][:, i] for i in range(n_hist)], conv_w[l],
                                 hist_is_bcx=False)
            xs = _linear(gs, conv_w_out, l, res=xs, hp=True)
            conv_p.append(stp)
            conv_s.append(sts)
        else:
            j = l - n_a
            qp = _linear(xp, attn_w_q, j, gamma=norm_mix[l], out_dtypes=(BF16,))
            ap = _sb_prompt(qp.reshape(bp, tp, D), kp_b.reshape(bp, tp, D), vp_b.reshape(bp, tp, D),
                            attn_b[j], n_heads=n_heads)
            xp = _linear(ap.reshape(bp * tp, D), attn_w_o, j, res=xp)
            qs = _linear(xs, attn_w_q, j, gamma=norm_mix[l], hp=True)
            a_s = _sb_decode(qs, cache_k, cache_v, page_table, attn_b[j])
            xs = _linear(a_s, attn_w_o, j, res=xs, hp=True)
        moe_args = (l, norm_ffn[l], wt[l], rb[l], expert_w_gate, expert_w_up, expert_w_down)
        xp = _moe(xp, *moe_args, n_groups=n_groups, epg=epg)
        xs = _moe_hp(xs, *moe_args, n_groups=n_groups, epg=epg)
        if l == n_a - 1:
            kp, kp_b = _linear(xp, w_kv3, 0, gamma=kv_norm, col_off=0, n_cols=D, out_dtypes=(F32, BF16))
            vp, vp_b = _linear(xp, w_kv3, 0, gamma=kv_norm, col_off=D, n_cols=D, out_dtypes=(F32, BF16))
            ks_new = _linear(xs, w_kv3, 0, gamma=kv_norm, col_off=0, n_cols=D, hp=True)
            vs_new = _linear(xs, w_kv3, 0, gamma=kv_norm, col_off=D, n_cols=D, hp=True)
    dh = D // n_heads
    y_prompt = _rmsnorm(xp, final_norm).reshape(bp, tp, D)
    y_sample = _rmsnorm(xs, final_norm).reshape(bs, ts, D)
    return (y_prompt, y_sample, jnp.stack(conv_p, axis=0), jnp.stack(conv_s, axis=0),
            kp.reshape(bp, tp, n_heads, dh), vp.reshape(bp, tp, n_heads, dh),
            ks_new.reshape(bs, ts, n_heads, dh), vs_new.reshape(bs, ts, n_heads, dh))
```
